```python
import jax, jax.numpy as jnp
from jax import lax
import numpy as np

D_MODEL = 2048
BATCH = 8
SEQ = 2048
DEPTH = 2

CHUNK = 64
Q_BLOCK = 128
NORM_EPS = 1e-6
ROPE_THETA = 10000.0
MASK_VALUE = -1e30

HG_HEADS = 4
HG_DK = 128
HG_DV = 128
HG_WIDTH = HG_HEADS * HG_DV

SB_HEADS = 6
SB_DH = 128
SB_WIDTH = SB_HEADS * SB_DH

MLA_HEADS = 6
MLA_NOPE = 128
MLA_ROPE = 64
MLA_V = 128
MLA_Q_RANK = 512
MLA_KV_RANK = 256
MLA_WIDTH = MLA_HEADS * MLA_V

MIX_WIDTH = HG_WIDTH + SB_WIDTH + MLA_WIDTH
IN_SIZES = (HG_HEADS * HG_DK, HG_HEADS * HG_DK, HG_WIDTH, HG_WIDTH,
            SB_WIDTH, SB_WIDTH, SB_WIDTH, MLA_Q_RANK, MLA_KV_RANK + MLA_ROPE)
IN_COLS = 4 * HG_WIDTH + 3 * SB_WIDTH + MLA_Q_RANK + MLA_KV_RANK + MLA_ROPE
D_FF = ((8 * D_MODEL + 3 * 256 - 1) // (3 * 256)) * 256

kernel_name = 'hybrid_hgrn2_stickbreak_mla_block'


def rmsnorm(x, g):
    xf = x.astype(jnp.float32)
    y = xf * lax.rsqrt(jnp.mean(xf * xf, axis=-1, keepdims=True) + NORM_EPS)
    return (y * g.astype(jnp.float32)).astype(x.dtype)


def group_rmsnorm(x, g, n_heads):
    b, s, w = x.shape
    xf = x.astype(jnp.float32).reshape(b, s, n_heads, w // n_heads)
    y = xf * lax.rsqrt(jnp.mean(xf * xf, axis=-1, keepdims=True) + NORM_EPS)
    return (y.reshape(b, s, w) * g.astype(jnp.float32)).astype(x.dtype)


def rope_angles(positions):
    inv_freq = ROPE_THETA ** (-jnp.arange(0, MLA_ROPE, 2, dtype=jnp.float32) / MLA_ROPE)
    ang = positions.astype(jnp.float32)[..., None] * inv_freq
    return jnp.cos(ang), jnp.sin(ang)


def apply_rope(x, cos, sin):
    xf = x.astype(jnp.float32)
    x1, x2 = xf[..., : MLA_ROPE // 2], xf[..., MLA_ROPE // 2:]
    return jnp.concatenate([x1 * cos - x2 * sin, x2 * cos + x1 * sin], axis=-1).astype(x.dtype)


def hgrn2_mixer(q, f_pre, v, lower_bound):
    b, s, _ = q.shape
    n_chunks = s // CHUNK
    lb = lower_bound.astype(jnp.float32)
    fp = f_pre.astype(jnp.float32)
    f = lb + (1.0 - lb) * jax.nn.sigmoid(fp)
    log_f = jnp.log(f)
    k = (1.0 - lb) * jax.nn.sigmoid(-fp)

    def to_chunks(t, d):
        return t.astype(jnp.float32).reshape(b, n_chunks, CHUNK, HG_HEADS, d).transpose(1, 0, 3, 2, 4)

    qc, kc, vc = to_chunks(q, HG_DK), to_chunks(k, HG_DK), to_chunks(v, HG_DV)
    gc = jnp.cumsum(to_chunks(log_f, HG_DK), axis=3)
    causal = jnp.tril(jnp.ones((CHUNK, CHUNK), dtype=bool))[:, :, None]

    def step(state, inp):
        q_c, k_c, v_c, g_c = inp
        diff = g_c[:, :, :, None, :] - g_c[:, :, None, :, :]
        decay = jnp.where(causal, jnp.exp(jnp.where(causal, diff, 0.0)), 0.0)
        scores = jnp.einsum('bhtk,bhsk,bhtsk->bhts', q_c, k_c, decay)
        o = jnp.einsum('bhts,bhsv->bhtv', scores, v_c) + jnp.einsum('bhtk,bhkv->bhtv', q_c * jnp.exp(g_c), state)
        g_last = g_c[:, :, -1:, :]
        state = jnp.exp(g_last)[:, :, 0, :, None] * state + jnp.einsum('bhsk,bhsv->bhkv', k_c * jnp.exp(g_last - g_c), v_c)
        return state, o

    s0 = jnp.zeros((b, HG_HEADS, HG_DK, HG_DV), jnp.float32)
    _, o = lax.scan(step, s0, (qc, kc, vc, gc))
    return o.transpose(1, 0, 3, 2, 4).reshape(b, s, HG_WIDTH).astype(q.dtype)


def stick_breaking_mixer(q, k, v):
    b, s, h, d = q.shape
    scale = d ** -0.5
    outs = []
    for blk in range(s // Q_BLOCK):
        t0, t1 = blk * Q_BLOCK, (blk + 1) * Q_BLOCK
        z = jnp.einsum('bthd,bshd->bhts', q[:, t0:t1], k[:, :t1]).astype(jnp.float32) * scale
        strict = jnp.arange(t1)[None, :] < jnp.arange(t0, t1)[:, None]
        log_keep = jnp.where(strict, jax.nn.log_sigmoid(-z), 0.0)
        between = lax.cumsum(log_keep, axis=3, reverse=True) - log_keep
        a = jnp.where(strict, jnp.exp(jnp.where(strict, jax.nn.log_sigmoid(z) + between, 0.0)), 0.0)
        outs.append(jnp.einsum('bhts,bshd->bthd', a, v[:, :t1].astype(jnp.float32)))
    return jnp.concatenate(outs, axis=1).reshape(b, s, h * d).astype(q.dtype)


def mla_mixer(c_q, c_kv_rope, positions, q_norm_g, w_uq, kv_norm_g, w_ukv):
    b, s, _ = c_q.shape
    q = (rmsnorm(c_q, q_norm_g) @ w_uq).reshape(b, s, MLA_HEADS, MLA_NOPE + MLA_ROPE)
    q_nope, q_rope = q[..., :MLA_NOPE], q[..., MLA_NOPE:]
    c_kv, k_rope = c_kv_rope[..., :MLA_KV_RANK], c_kv_rope[..., MLA_KV_RANK:]
    kv = (rmsnorm(c_kv, kv_norm_g) @ w_ukv).reshape(b, s, MLA_HEADS, MLA_NOPE + MLA_V)
    k_nope, v = kv[..., :MLA_NOPE], kv[..., MLA_NOPE:]
    cos, sin = rope_angles(positions)
    q_rope = apply_rope(q_rope, cos[:, :, None, :], sin[:, :, None, :])
    k_rope = apply_rope(k_rope, cos, sin)
    scale = (MLA_NOPE + MLA_ROPE) ** -0.5
    chunk_id = jnp.arange(s) // CHUNK
    outs = []
    for blk in range(s // Q_BLOCK):
        t0, t1 = blk * Q_BLOCK, (blk + 1) * Q_BLOCK
        sc = (jnp.einsum('bthd,bshd->bhts', q_nope[:, t0:t1], k_nope[:, :t1])
              + jnp.einsum('bthr,bsr->bhts', q_rope[:, t0:t1], k_rope[:, :t1])).astype(jnp.float32) * scale
        allowed = chunk_id[None, :t1] <= chunk_id[t0:t1, None]
        p = jax.nn.softmax(jnp.where(allowed, sc, MASK_VALUE), axis=-1)
        outs.append(jnp.einsum('bhts,bshd->bthd', p, v[:, :t1].astype(jnp.float32)))
    return jnp.concatenate(outs, axis=1).reshape(b, s, MLA_WIDTH).astype(c_q.dtype)


def setup_inputs(seed: int = 0) -> dict:
    key = jax.random.key(seed)
    ks = jax.random.split(key, 20)
    f32 = jnp.float32

    def w(k, shape, fan_in):
        return jax.random.normal(k, shape, f32) * (fan_in ** -0.5)

    def gain(k, shape):
        return 1.0 + 0.02 * jax.random.normal(k, shape, f32)

    x = jax.random.normal(ks[0], (BATCH, SEQ, D_MODEL), f32)
    offsets = jax.random.randint(ks[1], (BATCH, 1), 0, 64, dtype=jnp.int32) * CHUNK
    positions = (offsets + jnp.arange(SEQ, dtype=jnp.int32)[None, :]).astype(jnp.int32)
    return {
        'x': x,
        'positions': positions,
        'attn_norm_g': gain(ks[2], (DEPTH, D_MODEL)),
        'w_in': w(ks[3], (DEPTH, D_MODEL, IN_COLS), D_MODEL),
        'hg_lower_bounds': 1.0 + 0.1 * jax.random.normal(ks[4], (DEPTH, HG_HEADS * HG_DK), f32),
        'hg_norm_g': gain(ks[5], (DEPTH, HG_WIDTH)),
        'sb_norm_g': gain(ks[6], (DEPTH, SB_WIDTH)),
        'mla_q_norm_g': gain(ks[7], (DEPTH, MLA_Q_RANK)),
        'mla_w_uq': w(ks[8], (DEPTH, MLA_Q_RANK, MLA_HEADS * (MLA_NOPE + MLA_ROPE)), MLA_Q_RANK),
        'mla_kv_norm_g': gain(ks[9], (DEPTH, MLA_KV_RANK)),
        'mla_w_ukv': w(ks[10], (DEPTH, MLA_KV_RANK, MLA_HEADS * (MLA_NOPE + MLA_V)), MLA_KV_RANK),
        'mla_out_norm_g': gain(ks[11], (DEPTH, MLA_WIDTH)),
        'w_o': w(ks[12], (DEPTH, MIX_WIDTH, D_MODEL), MIX_WIDTH),
        'ffn_norm_g': gain(ks[13], (DEPTH, D_MODEL)),
        'w_gate': w(ks[14], (DEPTH, D_MODEL, D_FF), D_MODEL),
        'w_up': w(ks[15], (DEPTH, D_MODEL, D_FF), D_MODEL),
        'w_down': w(ks[16], (DEPTH, D_FF, D_MODEL), D_FF),
        'final_norm_g': gain(ks[17], (D_MODEL,)),
    }


def reference(x, positions, attn_norm_g, w_in, hg_lower_bounds, hg_norm_g, sb_norm_g,
              mla_q_norm_g, mla_w_uq, mla_kv_norm_g, mla_w_ukv, mla_out_norm_g, w_o,
              ffn_norm_g, w_gate, w_up, w_down, final_norm_g):
    b, s, _ = x.shape
    sm = jax.nn.softmax(hg_lower_bounds.astype(jnp.float32), axis=0)
    lower_bounds = jnp.cumsum(sm, axis=0) - sm[0:1]
    split_at = np.cumsum(IN_SIZES)[:-1].tolist()
    for layer in range(DEPTH):
        h = rmsnorm(x, attn_norm_g[layer])
        hq, hf, hi, hg, sq, sk, sv, c_q, c_kv_rope = jnp.split(h @ w_in[layer], split_at, axis=-1)

        hg_out = hgrn2_mixer(hq, hf, hi, lower_bounds[layer])
        hg_out = group_rmsnorm(hg_out, hg_norm_g[layer], HG_HEADS) * jax.nn.silu(hg)

        sb_out = stick_breaking_mixer(sq.reshape(b, s, SB_HEADS, SB_DH), sk.reshape(b, s, SB_HEADS, SB_DH),
                                      sv.reshape(b, s, SB_HEADS, SB_DH))
        sb_out = group_rmsnorm(sb_out, sb_norm_g[layer], SB_HEADS)

        mla_out = mla_mixer(c_q, c_kv_rope, positions, mla_q_norm_g[layer], mla_w_uq[layer],
                            mla_kv_norm_g[layer], mla_w_ukv[layer])
        mla_out = group_rmsnorm(mla_out, mla_out_norm_g[layer], MLA_HEADS)

        x = x + jnp.concatenate([hg_out, sb_out, mla_out], axis=-1) @ w_o[layer]

        h = rmsnorm(x, ffn_norm_g[layer])
        x = x + (jax.nn.silu(h @ w_gate[layer]) * (h @ w_up[layer])) @ w_down[layer]
    return rmsnorm(x, final_norm_g)
```

```python
import functools

import jax
import jax.numpy as jnp
from jax import lax
from jax.experimental import pallas as pl
from jax.experimental.pallas import tpu as pltpu

F32 = jnp.float32
BF16 = jnp.bfloat16

NORM_EPS = 1e-6
ROPE_THETA = 10000.0
CHUNK = 64

HG_HEADS = 4
HG_D = 128
SB_HEADS = 6
SB_D = 128
MLA_HEADS = 6
MLA_NOPE = 128
MLA_ROPE = 64
MLA_V = 128
MLA_Q_RANK = 512
MLA_KV_RANK = 256
MLA_QK_PAD = 256

HG_CHUNK = 64
SUBLANES = 8
LANES = 128
VMEM_LIMIT = 52 * 1024 * 1024


def _cparams(semantics):
    return pltpu.CompilerParams(dimension_semantics=semantics, vmem_limit_bytes=VMEM_LIMIT)


def _rms(x, g):
    return x * lax.rsqrt(jnp.mean(x * x, axis=-1, keepdims=True) + NORM_EPS) * g


def _dot(a, b):
    return jnp.dot(a, b, preferred_element_type=F32)


def _dot_nt(a, b):
    return lax.dot_general(a, b, (((1,), (1,)), ((), ())), preferred_element_type=F32)


def _dot_tn(a, b):
    return lax.dot_general(a, b, (((0,), (0,)), ((), ())), preferred_element_type=F32)


def _norm_proj_kernel(x_ref, g_ref, w_ref, o_ref, h_ref):
    @pl.when(pl.program_id(1) == 0)
    def _():
        h_ref[...] = _rms(x_ref[...], g_ref[...]).astype(BF16)

    o_ref[...] = _dot(h_ref[...], w_ref[...]).astype(o_ref.dtype)


def _norm_proj(x, g, w, out_dtype, tm, tn):
    t, d = x.shape
    n = w.shape[1]
    return pl.pallas_call(
        _norm_proj_kernel,
        grid=(t // tm, n // tn),
        in_specs=[
            pl.BlockSpec((tm, d), lambda i, j: (i, 0)),
            pl.BlockSpec((1, d), lambda i, j: (0, 0)),
            pl.BlockSpec((d, tn), lambda i, j: (0, j)),
        ],
        out_specs=pl.BlockSpec((tm, tn), lambda i, j: (i, j)),
        out_shape=jax.ShapeDtypeStruct((t, n), out_dtype),
        scratch_shapes=[pltpu.VMEM((tm, d), BF16)],
        compiler_params=_cparams(("parallel", "arbitrary")),
        name="norm_proj",
    )(x, g, w)


def _mla_proj_kernel(x_ref, g_ref, win_ref, gq_ref, wa_ref, wb_ref, gkv_ref, wk_ref, wv_ref,
                     cq_ref, sq_ref, ck_ref, sk_ref, q_ref, k_ref, v_ref):
    h = _rms(x_ref[...], g_ref[...]).astype(BF16)
    p = _dot(h, win_ref[...])
    c_q = p[:, :MLA_Q_RANK]
    c_kv = p[:, MLA_Q_RANK:MLA_Q_RANK + MLA_KV_RANK]
    k_rope = p[:, MLA_Q_RANK + MLA_KV_RANK:MLA_Q_RANK + MLA_KV_RANK + LANES]
    k_rope_rot = p[:, MLA_Q_RANK + MLA_KV_RANK + LANES:]

    cqn = _rms(c_q, gq_ref[...]).astype(BF16)
    qa = _dot(cqn, wa_ref[...])
    qb = _dot(cqn, wb_ref[...])
    cq = cq_ref[...]
    sq = sq_ref[...]
    for hd in range(MLA_HEADS):
        sl = slice(hd * MLA_QK_PAD, (hd + 1) * MLA_QK_PAD)
        q_ref[:, sl] = (qa[:, sl] * cq + qb[:, sl] * sq).astype(BF16)

    ckvn = _rms(c_kv, gkv_ref[...]).astype(BF16)
    kn = _dot(ckvn, wk_ref[...])
    v_ref[...] = _dot(ckvn, wv_ref[...]).astype(BF16)
    kr = (k_rope * ck_ref[...] + k_rope_rot * sk_ref[...]).astype(BF16)
    for hd in range(MLA_HEADS):
        k_ref[:, hd * MLA_QK_PAD:hd * MLA_QK_PAD + MLA_NOPE] = (
            kn[:, hd * MLA_NOPE:(hd + 1) * MLA_NOPE].astype(BF16))
        k_ref[:, hd * MLA_QK_PAD + MLA_NOPE:(hd + 1) * MLA_QK_PAD] = kr


def _mla_proj(x, g, w_in, gq, wa, wb, gkv, wk, wv, cq, sq, ck, sk, tm):
    t, d = x.shape
    const = lambda i: (0, 0)
    row = lambda i: (i, 0)
    full = lambda a: pl.BlockSpec(a.shape, const)
    qk_w = MLA_HEADS * MLA_QK_PAD
    v_w = MLA_HEADS * MLA_V
    return pl.pallas_call(
        _mla_proj_kernel,
        grid=(t // tm,),
        in_specs=[
            pl.BlockSpec((tm, d), row), full(g), full(w_in), full(gq), full(wa), full(wb),
            full(gkv), full(wk), full(wv),
            pl.BlockSpec((tm, MLA_QK_PAD), row), pl.BlockSpec((tm, MLA_QK_PAD), row),
            pl.BlockSpec((tm, LANES), row), pl.BlockSpec((tm, LANES), row),
        ],
        out_specs=[
            pl.BlockSpec((tm, qk_w), row), pl.BlockSpec((tm, qk_w), row), pl.BlockSpec((tm, v_w), row),
        ],
        out_shape=[
            jax.ShapeDtypeStruct((t, qk_w), BF16), jax.ShapeDtypeStruct((t, qk_w), BF16),
            jax.ShapeDtypeStruct((t, v_w), BF16),
        ],
        compiler_params=_cparams(("parallel",)),
        name="mla_proj",
    )(x, g, w_in, gq, wa, wb, gkv, wk, wv, cq, sq, ck, sk)


def _hgrn_intra_scores(q, g, g_ref, k_ref):
    c = HG_CHUNK
    lane = lax.broadcasted_iota(jnp.int32, (SUBLANES, LANES), 1)
    row = lax.broadcasted_iota(jnp.int32, (SUBLANES, LANES), 0)
    row1 = lax.broadcasted_iota(jnp.int32, (SUBLANES, 1), 0)
    out = []
    for tv in range(c // SUBLANES):
        t0 = tv * SUBLANES
        qv = q[t0:t0 + SUBLANES]
        gv = g[t0:t0 + SUBLANES]
        acc = jnp.zeros((SUBLANES, LANES), F32)
        for s in range(t0 + SUBLANES):
            d = gv - g_ref[s:s + 1, :]
            if s >= t0:
                d = jnp.where(row + t0 >= s, d, 0.0)
            m = qv * jnp.exp(d) * k_ref[s:s + 1, :]
            r = jnp.sum(m, axis=-1, keepdims=True)
            if s >= t0:
                r = jnp.where(row1 + t0 >= s, r, 0.0)
            acc = jnp.where(lane == s, r, acc)
        out.append(acc)
    return jnp.concatenate(out, axis=0)


def _hgrn_kernel(q_ref, f_ref, v_ref, gate_ref, lb_ref, gn_ref, o_ref, st_ref, g_sc, k_sc):
    c = HG_CHUNK
    n_chunks = q_ref.shape[0] // c

    @pl.when(pl.program_id(2) == 0)
    def _():
        st_ref[...] = jnp.zeros_like(st_ref)

    lb = lb_ref[...]
    gn = gn_ref[...]
    tri = (lax.broadcasted_iota(jnp.int32, (c, c), 0)
           >= lax.broadcasted_iota(jnp.int32, (c, c), 1)).astype(BF16)

    def chunk(ci, _):
        r0 = pl.multiple_of(ci * c, c)
        rows = pl.ds(r0, c)
        q = q_ref[rows, :]
        fp = f_ref[rows, :]
        v = v_ref[rows, :].astype(BF16)

        e = jnp.exp(-jnp.abs(fp))
        r = 1.0 / (1.0 + e)
        sig = jnp.where(fp >= 0, r, e * r)
        nsig = jnp.where(fp >= 0, e * r, r)
        log_f = jnp.log(lb + (1.0 - lb) * sig)
        k = (1.0 - lb) * nsig

        p1 = log_f.astype(BF16)
        r1 = log_f - p1.astype(F32)
        p2 = r1.astype(BF16)
        p3 = (r1 - p2.astype(F32)).astype(BF16)
        g = _dot(tri, p1) + _dot(tri, p2) + _dot(tri, p3)

        g_sc[...] = g
        k_sc[...] = k
        a = _hgrn_intra_scores(q, g, g_sc, k_sc)

        st = st_ref[...]
        g_last = g[c - 1:c, :]
        o = _dot(a[:, :c].astype(BF16), v) + _dot_nt((q * jnp.exp(g)).astype(BF16), st.astype(BF16))
        kd = (k * jnp.exp(g_last - g)).astype(BF16)
        st_ref[...] = st * jnp.exp(g_last) + _dot_tn(v, kd)

        y = _rms(o, gn)
        gate = gate_ref[rows, :]
        o_ref[rows, :] = (y * gate * (1.0 / (1.0 + jnp.exp(-gate)))).astype(o_ref.dtype)
        return 0

    lax.fori_loop(0, n_chunks, chunk, 0)


def _hgrn(proj, lb, gn, batch, tb):
    t = proj.shape[0]
    nt = t // batch // tb
    col = lambda off: pl.BlockSpec((tb, HG_D), lambda b, h, i: (b * nt + i, off + h))
    head_vec = pl.BlockSpec((1, HG_D), lambda b, h, i: (0, h))
    return pl.pallas_call(
        _hgrn_kernel,
        grid=(batch, HG_HEADS, nt),
        in_specs=[col(0), col(HG_HEADS), col(2 * HG_HEADS), col(3 * HG_HEADS), head_vec, head_vec],
        out_specs=pl.BlockSpec((tb, HG_D), lambda b, h, i: (b * nt + i, h)),
        out_shape=jax.ShapeDtypeStruct((t, HG_HEADS * HG_D), BF16),
        scratch_shapes=[pltpu.VMEM((HG_D, HG_D), F32), pltpu.VMEM((HG_CHUNK, HG_D), F32),
                        pltpu.VMEM((HG_CHUNK, HG_D), F32)],
        compiler_params=_cparams(("parallel", "parallel", "arbitrary")),
        name="hgrn2",
    )(proj, proj, proj, proj, lb, gn)


def _sb_kernel(q_ref, k_ref, v_ref, u_ref, gn_ref, o_ref, *, blk, scale):
    i = pl.program_id(2)
    q = q_ref[...]
    u = u_ref[...]
    row = lax.broadcasted_iota(jnp.int32, (blk, blk), 0)
    col = lax.broadcasted_iota(jnp.int32, (blk, blk), 1)
    strict = col < row

    def block(kb, later, acc, mask):
        rows = pl.ds(pl.multiple_of(kb * blk, blk), blk)
        z = _dot_nt(q, k_ref[rows, :]) * scale
        log_keep = -(jnp.maximum(z, 0.0) + jnp.log1p(jnp.exp(-jnp.abs(z))))
        log_beta = z + log_keep
        if mask is not None:
            log_keep = jnp.where(mask, log_keep, 0.0)
        hi = log_keep.astype(BF16)
        lo = (log_keep - hi.astype(F32)).astype(BF16)
        within = _dot(hi, u) + _dot(lo, u)
        a = jnp.exp(log_beta + within + later)
        if mask is not None:
            a = jnp.where(mask, a, 0.0)
        acc = acc + _dot(a.astype(BF16), v_ref[rows, :])
        later = later + jnp.sum(log_keep, axis=-1, keepdims=True)
        return later, acc

    later, acc = block(i, jnp.zeros((blk, 1), F32), jnp.zeros((blk, SB_D), F32), strict)

    def body(j, carry):
        return block(i - 1 - j, carry[0], carry[1], None)

    later, acc = lax.fori_loop(0, i, body, (later, acc))
    o_ref[...] = _rms(acc, gn_ref[...]).astype(o_ref.dtype)


def _sb_attention(qkv, gn, batch, blk):
    t = qkv.shape[0]
    seq = t // batch
    nq = seq // blk
    u = (lax.broadcasted_iota(jnp.int32, (blk, blk), 0)
         > lax.broadcasted_iota(jnp.int32, (blk, blk), 1)).astype(BF16)
    q_spec = pl.BlockSpec((blk, SB_D), lambda b, h, i: (b * nq + i, h))
    k_spec = pl.BlockSpec((seq, SB_D), lambda b, h, i: (b, SB_HEADS + h))
    v_spec = pl.BlockSpec((seq, SB_D), lambda b, h, i: (b, 2 * SB_HEADS + h))
    return pl.pallas_call(
        functools.partial(_sb_kernel, blk=blk, scale=SB_D ** -0.5),
        grid=(batch, SB_HEADS, nq),
        in_specs=[q_spec, k_spec, v_spec, pl.BlockSpec((blk, blk), lambda b, h, i: (0, 0)),
                  pl.BlockSpec((1, SB_D), lambda b, h, i: (0, h))],
        out_specs=q_spec,
        out_shape=jax.ShapeDtypeStruct((t, SB_HEADS * SB_D), BF16),
        compiler_params=_cparams(("parallel", "parallel", "arbitrary")),
        name="sb_attention",
    )(qkv, qkv, qkv, u, gn)


def _mla_kernel(q_ref, k_ref, v_ref, gn_ref, o_ref, *, blk):
    i = pl.program_id(2)
    q = q_ref[...]
    row = lax.broadcasted_iota(jnp.int32, (blk, blk), 0)
    col = lax.broadcasted_iota(jnp.int32, (blk, blk), 1)
    allowed = (col // CHUNK) <= (row // CHUNK)

    def block(kb, m, l, acc, mask):
        rows = pl.ds(pl.multiple_of(kb * blk, blk), blk)
        s = _dot_nt(q, k_ref[rows, :])
        if mask is not None:
            s = jnp.where(mask, s, -jnp.inf)
        m_new = jnp.maximum(m, jnp.max(s, axis=-1, keepdims=True))
        alpha = jnp.exp(m - m_new)
        p = jnp.exp(s - m_new)
        l = alpha * l + jnp.sum(p, axis=-1, keepdims=True)
        acc = alpha * acc + _dot(p.astype(BF16), v_ref[rows, :])
        return m_new, l, acc

    m, l, acc = block(i, jnp.full((blk, 1), -jnp.inf, F32), jnp.zeros((blk, 1), F32),
                      jnp.zeros((blk, MLA_V), F32), allowed)

    def body(j, carry):
        return block(j, carry[0], carry[1], carry[2], None)

    m, l, acc = lax.fori_loop(0, i, body, (m, l, acc))
    o_ref[...] = _rms(acc / l, gn_ref[...]).astype(o_ref.dtype)


def _mla_attention(q, k, v, gn, batch, blk):
    t = q.shape[0]
    seq = t // batch
    nq = seq // blk
    return pl.pallas_call(
        functools.partial(_mla_kernel, blk=blk),
        grid=(batch, MLA_HEADS, nq),
        in_specs=[
            pl.BlockSpec((blk, MLA_QK_PAD), lambda b, h, i: (b * nq + i, h)),
            pl.BlockSpec((seq, MLA_QK_PAD), lambda b, h, i: (b, h)),
            pl.BlockSpec((seq, MLA_V), lambda b, h, i: (b, h)),
            pl.BlockSpec((1, MLA_V), lambda b, h, i: (0, h)),
        ],
        out_specs=pl.BlockSpec((blk, MLA_V), lambda b, h, i: (b * nq + i, h)),
        out_shape=jax.ShapeDtypeStruct((t, MLA_HEADS * MLA_V), BF16),
        compiler_params=_cparams(("parallel", "parallel", "arbitrary")),
        name="mla_attention",
    )(q, k, v, gn)


def _out_proj_kernel(x_ref, a_ref, b_ref, c_ref, wa_ref, wb_ref, wc_ref, o_ref):
    o_ref[...] = (x_ref[...] + _dot(a_ref[...], wa_ref[...]) + _dot(b_ref[...], wb_ref[...])
                  + _dot(c_ref[...], wc_ref[...]))


def _out_proj(x, a, b, c, wa, wb, wc, tm, tn):
    t, d = x.shape
    act = lambda arr: pl.BlockSpec((tm, arr.shape[1]), lambda i, j: (i, 0))
    wgt = lambda arr: pl.BlockSpec((arr.shape[0], tn), lambda i, j: (0, j))
    tile = pl.BlockSpec((tm, tn), lambda i, j: (i, j))
    return pl.pallas_call(
        _out_proj_kernel,
        grid=(t // tm, d // tn),
        in_specs=[tile, act(a), act(b), act(c), wgt(wa), wgt(wb), wgt(wc)],
        out_specs=tile,
        out_shape=jax.ShapeDtypeStruct((t, d), F32),
        compiler_params=_cparams(("parallel", "parallel")),
        name="out_proj",
    )(x, a, b, c, wa, wb, wc)


def _ffn_kernel(x_ref, g_ref, wg_ref, wu_ref, wd_ref, gf_ref, o_ref, h_ref, *, final_norm):
    j = pl.program_id(1)

    @pl.when(j == 0)
    def _():
        x = x_ref[...]
        h_ref[...] = _rms(x, g_ref[...]).astype(BF16)
        o_ref[...] = x

    h = h_ref[...]
    gate = _dot(h, wg_ref[...])
    up = _dot(h, wu_ref[...])
    act = (gate * (1.0 / (1.0 + jnp.exp(-gate))) * up).astype(BF16)
    o_ref[...] += _dot(act, wd_ref[...])

    if final_norm:
        @pl.when(j == pl.num_programs(1) - 1)
        def _():
            o_ref[...] = _rms(o_ref[...], gf_ref[...])


def _ffn(x, g, wg, wu, wd, gf, final_norm, tm, tf):
    t, d = x.shape
    ff = wg.shape[1]
    row = pl.BlockSpec((tm, d), lambda i, j: (i, 0))
    vec = pl.BlockSpec((1, d), lambda i, j: (0, 0))
    return pl.pallas_call(
        functools.partial(_ffn_kernel, final_norm=final_norm),
        grid=(t // tm, ff // tf),
        in_specs=[row, vec, pl.BlockSpec((d, tf), lambda i, j: (0, j)),
                  pl.BlockSpec((d, tf), lambda i, j: (0, j)),
                  pl.BlockSpec((tf, d), lambda i, j: (j, 0)), vec],
        out_specs=row,
        out_shape=jax.ShapeDtypeStruct((t, d), F32),
        scratch_shapes=[pltpu.VMEM((tm, d), BF16)],
        compiler_params=_cparams(("parallel", "arbitrary")),
        name="ffn",
    )(x, g, wg, wu, wd, gf)


def _rotate_half_cols(w):
    half = MLA_ROPE // 2
    return jnp.concatenate([-w[..., half:], w[..., :half]], axis=-1)


def _rope_tables(positions, scale):
    inv_freq = ROPE_THETA ** (-jnp.arange(0, MLA_ROPE, 2, dtype=F32) / MLA_ROPE)
    ang = positions.astype(F32).reshape(-1, 1) * inv_freq
    cos2 = jnp.concatenate([jnp.cos(ang)] * 2, axis=-1)
    sin2 = jnp.concatenate([jnp.sin(ang)] * 2, axis=-1)
    t = ang.shape[0]
    pad = jnp.zeros((t, MLA_QK_PAD - MLA_NOPE - MLA_ROPE), F32)
    cq = jnp.concatenate([jnp.ones((t, MLA_NOPE), F32), cos2, pad], axis=-1) * scale
    sq = jnp.concatenate([jnp.zeros((t, MLA_NOPE), F32), sin2, pad], axis=-1) * scale
    ck = jnp.concatenate([cos2, jnp.zeros((t, LANES - MLA_ROPE), F32)], axis=-1)
    sk = jnp.concatenate([sin2, jnp.zeros((t, LANES - MLA_ROPE), F32)], axis=-1)
    return cq, sq, ck, sk


def _mla_weights(w_in_mla, w_uq, w_ukv):
    d = w_in_mla.shape[0]
    w_cq = w_in_mla[:, :MLA_Q_RANK]
    w_ckv = w_in_mla[:, MLA_Q_RANK:MLA_Q_RANK + MLA_KV_RANK]
    w_kr = w_in_mla[:, MLA_Q_RANK + MLA_KV_RANK:]
    zpad = jnp.zeros((d, LANES - MLA_ROPE), F32)
    w_in = jnp.concatenate([w_cq, w_ckv, w_kr, zpad, _rotate_half_cols(w_kr), zpad], axis=1)

    uq = w_uq.reshape(MLA_Q_RANK, MLA_HEADS, MLA_NOPE + MLA_ROPE)
    nope, rope = uq[..., :MLA_NOPE], uq[..., MLA_NOPE:]
    zr = jnp.zeros((MLA_Q_RANK, MLA_HEADS, MLA_QK_PAD - MLA_NOPE - MLA_ROPE), F32)
    wa = jnp.concatenate([nope, rope, zr], axis=-1).reshape(MLA_Q_RANK, MLA_HEADS * MLA_QK_PAD)
    wb = jnp.concatenate([jnp.zeros_like(nope), _rotate_half_cols(rope), zr], axis=-1)
    wb = wb.reshape(MLA_Q_RANK, MLA_HEADS * MLA_QK_PAD)

    ukv = w_ukv.reshape(MLA_KV_RANK, MLA_HEADS, MLA_NOPE + MLA_V)
    wk = ukv[..., :MLA_NOPE].reshape(MLA_KV_RANK, MLA_HEADS * MLA_NOPE)
    wv = ukv[..., MLA_NOPE:].reshape(MLA_KV_RANK, MLA_HEADS * MLA_V)
    return tuple(w.astype(BF16) for w in (w_in, wa, wb, wk, wv))


def kernel(x, positions, attn_norm_g, w_in, hg_lower_bounds, hg_norm_g, sb_norm_g, mla_q_norm_g,
           mla_w_uq, mla_kv_norm_g, mla_w_ukv, mla_out_norm_g, w_o, ffn_norm_g, w_gate, w_up,
           w_down, final_norm_g):
    batch, seq, d_model = x.shape
    depth = w_in.shape[0]
    t = batch * seq
    hg_w = HG_HEADS * HG_D
    sb_w = SB_HEADS * SB_D
    mla_w = MLA_HEADS * MLA_V
    hg_cols = 4 * hg_w
    sb_cols = 3 * sb_w

    sm = jax.nn.softmax(hg_lower_bounds.astype(F32), axis=0)
    lower_bounds = jnp.cumsum(sm, axis=0) - sm[0:1]
    tables = _rope_tables(positions, (MLA_NOPE + MLA_ROPE) ** -0.5)

    xt = x.reshape(t, d_model)
    for layer in range(depth):
        g_attn = attn_norm_g[layer].reshape(1, d_model)
        w_l = w_in[layer]
        w_hg = w_l[:, :hg_cols].astype(BF16)
        w_sb = w_l[:, hg_cols:hg_cols + sb_cols].astype(BF16)
        mla_ws = _mla_weights(w_l[:, hg_cols + sb_cols:], mla_w_uq[layer], mla_w_ukv[layer])

        hg_proj = _norm_proj(xt, g_attn, w_hg, F32, tm=min(t, 1024), tn=1024)
        sb_proj = _norm_proj(xt, g_attn, w_sb, BF16, tm=min(t, 1024), tn=sb_w)
        mq, mk, mv = _mla_proj(xt, g_attn, mla_ws[0], mla_q_norm_g[layer].reshape(1, -1), mla_ws[1],
                               mla_ws[2], mla_kv_norm_g[layer].reshape(1, -1), mla_ws[3], mla_ws[4],
                               *tables, tm=min(t, 512))

        hg_out = _hgrn(hg_proj, lower_bounds[layer].reshape(1, hg_w), hg_norm_g[layer].reshape(1, hg_w),
                       batch, tb=min(seq, 512))
        sb_out = _sb_attention(sb_proj, sb_norm_g[layer].reshape(1, sb_w), batch, blk=min(seq, 256))
        mla_out = _mla_attention(mq, mk, mv, mla_out_norm_g[layer].reshape(1, mla_w), batch,
                                 blk=min(seq, 256))

        wo = w_o[layer].astype(BF16)
        xt = _out_proj(xt, hg_out, sb_out, mla_out, wo[:hg_w], wo[hg_w:hg_w + sb_w], wo[hg_w + sb_w:],
                       tm=min(t, 1024), tn=1024)
        xt = _ffn(xt, ffn_norm_g[layer].reshape(1, d_model), w_gate[layer].astype(BF16),
                  w_up[layer].astype(BF16), w_down[layer].astype(BF16), final_norm_g.reshape(1, d_model),
                  final_norm=(layer == depth - 1), tm=min(t, 512), tf=512)
    return xt.reshape(batch, seq, d_model)
```

```python
import functools

import jax
import jax.numpy as jnp
from jax import lax
from jax.experimental import pallas as pl
from jax.experimental.pallas import tpu as pltpu

F32 = jnp.float32
BF16 = jnp.bfloat16

NORM_EPS = 1e-6
ROPE_THETA = 10000.0
CHUNK = 64

HG_HEADS = 4
HG_D = 128
SB_HEADS = 6
SB_D = 128
MLA_HEADS = 6
MLA_NOPE = 128
MLA_ROPE = 64
MLA_V = 128
MLA_Q_RANK = 512
MLA_KV_RANK = 256
MLA_QK_PAD = 256

ATTN_HEADS_PER_STEP = 6
HG_CHUNK = 64
HG_SUB = 16
SUBLANES = 8
LANES = 128
VMEM_LIMIT = 52 * 1024 * 1024


def _cparams(semantics):
    return pltpu.CompilerParams(dimension_semantics=semantics, vmem_limit_bytes=VMEM_LIMIT)


def _rms(x, g):
    return x * lax.rsqrt(jnp.mean(x * x, axis=-1, keepdims=True) + NORM_EPS) * g


def _dot(a, b):
    return jnp.dot(a, b, preferred_element_type=F32)


def _dot_nt(a, b):
    return lax.dot_general(a, b, (((1,), (1,)), ((), ())), preferred_element_type=F32)


def _dot_tn(a, b):
    return lax.dot_general(a, b, (((0,), (0,)), ((), ())), preferred_element_type=F32)


def _norm_proj_kernel(x_ref, g_ref, w_ref, o_ref, h_ref):
    @pl.when(pl.program_id(1) == 0)
    def _():
        h_ref[...] = _rms(x_ref[...], g_ref[...]).astype(BF16)

    o_ref[...] = _dot(h_ref[...], w_ref[...]).astype(o_ref.dtype)


def _norm_proj(x, g, w, out_dtype, tm, tn):
    t, d = x.shape
    n = w.shape[1]
    return pl.pallas_call(
        _norm_proj_kernel,
        grid=(t // tm, n // tn),
        in_specs=[
            pl.BlockSpec((tm, d), lambda i, j: (i, 0)),
            pl.BlockSpec((1, d), lambda i, j: (0, 0)),
            pl.BlockSpec((d, tn), lambda i, j: (0, j)),
        ],
        out_specs=pl.BlockSpec((tm, tn), lambda i, j: (i, j)),
        out_shape=jax.ShapeDtypeStruct((t, n), out_dtype),
        scratch_shapes=[pltpu.VMEM((tm, d), BF16)],
        compiler_params=_cparams(("parallel", "arbitrary")),
        name="norm_proj",
    )(x, g, w)


def _mla_proj_kernel(x_ref, g_ref, win_ref, gq_ref, wa_ref, wb_ref, gkv_ref, wk_ref, wv_ref,
                     cq_ref, sq_ref, ck_ref, sk_ref, q_ref, k_ref, v_ref):
    h = _rms(x_ref[...], g_ref[...]).astype(BF16)
    p = _dot(h, win_ref[...])
    c_q = p[:, :MLA_Q_RANK]
    c_kv = p[:, MLA_Q_RANK:MLA_Q_RANK + MLA_KV_RANK]
    k_rope = p[:, MLA_Q_RANK + MLA_KV_RANK:MLA_Q_RANK + MLA_KV_RANK + LANES]
    k_rope_rot = p[:, MLA_Q_RANK + MLA_KV_RANK + LANES:]

    cqn = _rms(c_q, gq_ref[...]).astype(BF16)
    qa = _dot(cqn, wa_ref[...])
    qb = _dot(cqn, wb_ref[...])
    cq = cq_ref[...]
    sq = sq_ref[...]
    for hd in range(MLA_HEADS):
        sl = slice(hd * MLA_QK_PAD, (hd + 1) * MLA_QK_PAD)
        q_ref[:, sl] = (qa[:, sl] * cq + qb[:, sl] * sq).astype(BF16)

    ckvn = _rms(c_kv, gkv_ref[...]).astype(BF16)
    kn = _dot(ckvn, wk_ref[...])
    v_ref[...] = _dot(ckvn, wv_ref[...]).astype(BF16)
    kr = (k_rope * ck_ref[...] + k_rope_rot * sk_ref[...]).astype(BF16)
    for hd in range(MLA_HEADS):
        k_ref[:, hd * MLA_QK_PAD:hd * MLA_QK_PAD + MLA_NOPE] = (
            kn[:, hd * MLA_NOPE:(hd + 1) * MLA_NOPE].astype(BF16))
        k_ref[:, hd * MLA_QK_PAD + MLA_NOPE:(hd + 1) * MLA_QK_PAD] = kr


def _mla_proj(x, g, w_in, gq, wa, wb, gkv, wk, wv, cq, sq, ck, sk, tm):
    t, d = x.shape
    const = lambda i: (0, 0)
    row = lambda i: (i, 0)
    full = lambda a: pl.BlockSpec(a.shape, const)
    qk_w = MLA_HEADS * MLA_QK_PAD
    v_w = MLA_HEADS * MLA_V
    return pl.pallas_call(
        _mla_proj_kernel,
        grid=(t // tm,),
        in_specs=[
            pl.BlockSpec((tm, d), row), full(g), full(w_in), full(gq), full(wa), full(wb),
            full(gkv), full(wk), full(wv),
            pl.BlockSpec((tm, MLA_QK_PAD), row), pl.BlockSpec((tm, MLA_QK_PAD), row),
            pl.BlockSpec((tm, LANES), row), pl.BlockSpec((tm, LANES), row),
        ],
        out_specs=[
            pl.BlockSpec((tm, qk_w), row), pl.BlockSpec((tm, qk_w), row), pl.BlockSpec((tm, v_w), row),
        ],
        out_shape=[
            jax.ShapeDtypeStruct((t, qk_w), BF16), jax.ShapeDtypeStruct((t, qk_w), BF16),
            jax.ShapeDtypeStruct((t, v_w), BF16),
        ],
        compiler_params=_cparams(("parallel",)),
        name="mla_proj",
    )(x, g, w_in, gq, wa, wb, gkv, wk, wv, cq, sq, ck, sk)


def _hgrn_diag_scores(q, g, g_ref, k_ref):
    lane = lax.broadcasted_iota(jnp.int32, (SUBLANES, LANES), 1)
    row = lax.broadcasted_iota(jnp.int32, (SUBLANES, LANES), 0)
    row1 = lax.broadcasted_iota(jnp.int32, (SUBLANES, 1), 0)
    out = []
    for tv in range(HG_CHUNK // SUBLANES):
        t0 = tv * SUBLANES
        s0 = (t0 // HG_SUB) * HG_SUB
        qv = q[t0:t0 + SUBLANES]
        gv = g[t0:t0 + SUBLANES]
        acc = jnp.zeros((SUBLANES, LANES), F32)
        for s in range(s0, t0 + SUBLANES):
            d = gv - g_ref[s:s + 1, :]
            if s >= t0:
                d = jnp.where(row + t0 >= s, d, 0.0)
            m = qv * jnp.exp(d) * k_ref[s:s + 1, :]
            r = jnp.sum(m, axis=-1, keepdims=True)
            if s >= t0:
                r = jnp.where(row1 + t0 >= s, r, 0.0)
            acc = jnp.where(lane == s, r, acc)
        out.append(acc)
    return jnp.concatenate(out, axis=0)


def _hgrn_kernel(q_ref, f_ref, v_ref, gate_ref, lb_ref, gn_ref, o_ref, st_ref, g_sc, k_sc):
    c = HG_CHUNK
    n_sub = c // HG_SUB
    n_chunks = q_ref.shape[0] // c
    heads = range(HG_HEADS)
    cols = [slice(hd * HG_D, (hd + 1) * HG_D) for hd in heads]

    @pl.when(pl.program_id(1) == 0)
    def _():
        st_ref[...] = jnp.zeros_like(st_ref)

    tri = (lax.broadcasted_iota(jnp.int32, (c, c), 0)
           >= lax.broadcasted_iota(jnp.int32, (c, c), 1)).astype(BF16)
    zeros_sub = jnp.zeros((HG_SUB, LANES), F32)

    def chunk(ci, _):
        rows = pl.ds(pl.multiple_of(ci * c, c), c)

        ks, pieces = [], []
        for cs in cols:
            fp = f_ref[rows, cs]
            lb = lb_ref[:, cs]
            e = jnp.exp(-jnp.abs(fp))
            r = 1.0 / (1.0 + e)
            sig = jnp.where(fp >= 0, r, e * r)
            nsig = jnp.where(fp >= 0, e * r, r)
            log_f = jnp.log(lb + (1.0 - lb) * sig)
            ks.append((1.0 - lb) * nsig)
            p1 = log_f.astype(BF16)
            r1 = log_f - p1.astype(F32)
            p2 = r1.astype(BF16)
            p3 = (r1 - p2.astype(F32)).astype(BF16)
            pieces.append(jnp.concatenate([p1, p2, p3], axis=1))
        gs = []
        for p in pieces:
            r = _dot(tri, p)
            gs.append(r[:, :HG_D] + r[:, HG_D:2 * HG_D] + r[:, 2 * HG_D:])

        qs, vs, q_in, k_out, q_hat, k_hat = [], [], [], [], [], []
        for hd, cs in enumerate(cols):
            g, k = gs[hd], ks[hd]
            g_sc[hd] = g
            k_sc[hd] = k
            q = q_ref[rows, cs]
            qs.append(q)
            vs.append(v_ref[rows, cs].astype(BF16))
            g_last = g_sc[hd, c - 1:c, :]
            q_in.append((q * jnp.exp(g)).astype(BF16))
            k_out.append((k * jnp.exp(g_last - g)).astype(BF16))
            qh, kh = [], []
            for sb in range(1, n_sub):
                n = sb * HG_SUB
                g_prev = g_sc[hd, n - 1:n, :]
                qh.append((q[n:n + HG_SUB] * jnp.exp(g[n:n + HG_SUB] - g_prev)).astype(BF16))
                kh.append(jnp.concatenate(
                    [k[:n] * jnp.exp(g_prev - g[:n])] + [zeros_sub] * (n_sub - sb), axis=0).astype(BF16))
            q_hat.append(qh)
            k_hat.append(kh)

        off, inter = [], []
        for hd in heads:
            off.append(jnp.concatenate(
                [zeros_sub[:, :c]] + [_dot_nt(qh, kh) for qh, kh in zip(q_hat[hd], k_hat[hd])], axis=0))
            st = st_ref[hd]
            inter.append(_dot_nt(q_in[hd], st.astype(BF16)))
            g_last = g_sc[hd, c - 1:c, :]
            st_ref[hd] = st * jnp.exp(g_last) + _dot_tn(vs[hd], k_out[hd])

        diag = [_hgrn_diag_scores(qs[hd], gs[hd], g_sc.at[hd], k_sc.at[hd]) for hd in heads]
        outs = [_dot((diag[hd][:, :c] + off[hd]).astype(BF16), vs[hd]) + inter[hd] for hd in heads]

        for hd, cs in enumerate(cols):
            y = _rms(outs[hd], gn_ref[:, cs])
            gate = gate_ref[rows, cs]
            o_ref[rows, cs] = (y * gate * (1.0 / (1.0 + jnp.exp(-gate)))).astype(o_ref.dtype)
        return 0

    lax.fori_loop(0, n_chunks, chunk, 0)


def _hgrn(proj, lb, gn, batch, tb):
    t = proj.shape[0]
    nt = t // batch // tb
    width = HG_HEADS * HG_D
    col = lambda off: pl.BlockSpec((tb, width), lambda b, i: (b * nt + i, off))
    vec = pl.BlockSpec((1, width), lambda b, i: (0, 0))
    per_head = lambda rows: pltpu.VMEM((HG_HEADS, rows, HG_D), F32)
    return pl.pallas_call(
        _hgrn_kernel,
        grid=(batch, nt),
        in_specs=[col(0), col(1), col(2), col(3), vec, vec],
        out_specs=pl.BlockSpec((tb, width), lambda b, i: (b * nt + i, 0)),
        out_shape=jax.ShapeDtypeStruct((t, width), BF16),
        scratch_shapes=[per_head(HG_D), per_head(HG_CHUNK), per_head(HG_CHUNK)],
        compiler_params=_cparams(("parallel", "arbitrary")),
        name="hgrn2",
    )(proj, proj, proj, proj, lb, gn)


def _sb_kernel(q_ref, k_ref, v_ref, u_ref, gn_ref, o_ref, *, blk, heads, scale):
    i = pl.program_id(2)
    u = u_ref[...]
    row = lax.broadcasted_iota(jnp.int32, (blk, blk), 0)
    col = lax.broadcasted_iota(jnp.int32, (blk, blk), 1)
    strict = col < row

    def blocks(kb, carry, mask):
        rows = pl.ds(pl.multiple_of(kb * blk, blk), blk)
        cols = [slice(hd * SB_D, (hd + 1) * SB_D) for hd in range(heads)]
        zs = [_dot_nt(q_ref[:, c], k_ref[rows, c]) for c in cols]
        stage = []
        for z in zs:
            z = z * scale
            log_keep = -(jnp.maximum(z, 0.0) + jnp.log(1.0 + jnp.exp(-jnp.abs(z))))
            log_beta = z + log_keep
            if mask is not None:
                log_keep = jnp.where(mask, log_keep, 0.0)
            hi = log_keep.astype(BF16)
            lo = (log_keep - hi.astype(F32)).astype(BF16)
            stage.append((log_beta, hi, lo, jnp.sum(log_keep, axis=-1, keepdims=True)))
        withins = [_dot(hi, u) + _dot(lo, u) for _, hi, lo, _ in stage]
        weights = []
        for (log_beta, _, _, _), within, (later, _) in zip(stage, withins, carry):
            a = jnp.exp(log_beta + within + later)
            if mask is not None:
                a = jnp.where(mask, a, 0.0)
            weights.append(a.astype(BF16))
        return tuple((later + st[3], acc + _dot(a, v_ref[rows, c]))
                     for st, a, c, (later, acc) in zip(stage, weights, cols, carry))

    carry = tuple((jnp.zeros((blk, 1), F32), jnp.zeros((blk, SB_D), F32)) for _ in range(heads))
    carry = blocks(i, carry, strict)
    carry = lax.fori_loop(0, i, lambda j, c: blocks(i - 1 - j, c, None), carry)
    for hd in range(heads):
        cols = slice(hd * SB_D, (hd + 1) * SB_D)
        o_ref[:, cols] = _rms(carry[hd][1], gn_ref[:, cols]).astype(o_ref.dtype)


def _sb_attention(qkv, gn, batch, blk, heads):
    t = qkv.shape[0]
    seq = t // batch
    nq = seq // blk
    groups = SB_HEADS // heads
    width = heads * SB_D
    u = (lax.broadcasted_iota(jnp.int32, (blk, blk), 0)
         > lax.broadcasted_iota(jnp.int32, (blk, blk), 1)).astype(BF16)
    q_spec = pl.BlockSpec((blk, width), lambda b, h, i: (b * nq + i, h))
    k_spec = pl.BlockSpec((seq, width), lambda b, h, i: (b, groups + h))
    v_spec = pl.BlockSpec((seq, width), lambda b, h, i: (b, 2 * groups + h))
    return pl.pallas_call(
        functools.partial(_sb_kernel, blk=blk, heads=heads, scale=SB_D ** -0.5),
        grid=(batch, groups, nq),
        in_specs=[q_spec, k_spec, v_spec, pl.BlockSpec((blk, blk), lambda b, h, i: (0, 0)),
                  pl.BlockSpec((1, width), lambda b, h, i: (0, h))],
        out_specs=q_spec,
        out_shape=jax.ShapeDtypeStruct((t, SB_HEADS * SB_D), BF16),
        compiler_params=_cparams(("parallel", "parallel", "arbitrary")),
        name="sb_attention",
    )(qkv, qkv, qkv, u, gn)


def _mla_kernel(q_ref, k_ref, v_ref, gn_ref, o_ref, *, blk, heads):
    i = pl.program_id(2)
    row = lax.broadcasted_iota(jnp.int32, (blk, blk), 0)
    col = lax.broadcasted_iota(jnp.int32, (blk, blk), 1)
    allowed = (col // CHUNK) <= (row // CHUNK)

    def blocks(kb, carry, mask):
        rows = pl.ds(pl.multiple_of(kb * blk, blk), blk)
        scores = [_dot_nt(q_ref[:, hd * MLA_QK_PAD:(hd + 1) * MLA_QK_PAD],
                          k_ref[rows, hd * MLA_QK_PAD:(hd + 1) * MLA_QK_PAD]) for hd in range(heads)]
        stage = []
        for s, (m, l, _) in zip(scores, carry):
            if mask is not None:
                s = jnp.where(mask, s, -jnp.inf)
            m_new = jnp.maximum(m, jnp.max(s, axis=-1, keepdims=True))
            alpha = jnp.exp(m - m_new)
            p = jnp.exp(s - m_new)
            stage.append((m_new, alpha, alpha * l + jnp.sum(p, axis=-1, keepdims=True), p.astype(BF16)))
        return tuple((m_new, l_new, alpha * acc + _dot(p, v_ref[rows, hd * MLA_V:(hd + 1) * MLA_V]))
                     for hd, ((m_new, alpha, l_new, p), (_, _, acc)) in enumerate(zip(stage, carry)))

    carry = tuple((jnp.full((blk, 1), -jnp.inf, F32), jnp.zeros((blk, 1), F32),
                   jnp.zeros((blk, MLA_V), F32)) for _ in range(heads))
    carry = blocks(i, carry, allowed)
    carry = lax.fori_loop(0, i, lambda j, c: blocks(j, c, None), carry)
    for hd in range(heads):
        cols = slice(hd * MLA_V, (hd + 1) * MLA_V)
        _, l, acc = carry[hd]
        o_ref[:, cols] = _rms(acc / l, gn_ref[:, cols]).astype(o_ref.dtype)


def _mla_attention(q, k, v, gn, batch, blk, heads):
    t = q.shape[0]
    seq = t // batch
    nq = seq // blk
    qk_w = heads * MLA_QK_PAD
    v_w = heads * MLA_V
    return pl.pallas_call(
        functools.partial(_mla_kernel, blk=blk, heads=heads),
        grid=(batch, MLA_HEADS // heads, nq),
        in_specs=[
            pl.BlockSpec((blk, qk_w), lambda b, h, i: (b * nq + i, h)),
            pl.BlockSpec((seq, qk_w), lambda b, h, i: (b, h)),
            pl.BlockSpec((seq, v_w), lambda b, h, i: (b, h)),
            pl.BlockSpec((1, v_w), lambda b, h, i: (0, h)),
        ],
        out_specs=pl.BlockSpec((blk, v_w), lambda b, h, i: (b * nq + i, h)),
        out_shape=jax.ShapeDtypeStruct((t, MLA_HEADS * MLA_V), BF16),
        compiler_params=_cparams(("parallel", "parallel", "arbitrary")),
        name="mla_attention",
    )(q, k, v, gn)


def _out_proj_kernel(x_ref, a_ref, b_ref, c_ref, wa_ref, wb_ref, wc_ref, o_ref):
    o_ref[...] = (x_ref[...] + _dot(a_ref[...], wa_ref[...]) + _dot(b_ref[...], wb_ref[...])
                  + _dot(c_ref[...], wc_ref[...]))


def _out_proj(x, a, b, c, wa, wb, wc, tm, tn):
    t, d = x.shape
    act = lambda arr: pl.BlockSpec((tm, arr.shape[1]), lambda i, j: (i, 0))
    wgt = lambda arr: pl.BlockSpec((arr.shape[0], tn), lambda i, j: (0, j))
    tile = pl.BlockSpec((tm, tn), lambda i, j: (i, j))
    return pl.pallas_call(
        _out_proj_kernel,
        grid=(t // tm, d // tn),
        in_specs=[tile, act(a), act(b), act(c), wgt(wa), wgt(wb), wgt(wc)],
        out_specs=tile,
        out_shape=jax.ShapeDtypeStruct((t, d), F32),
        compiler_params=_cparams(("parallel", "parallel")),
        name="out_proj",
    )(x, a, b, c, wa, wb, wc)


def _ffn_kernel(x_ref, g_ref, wg_ref, wu_ref, wd_ref, gf_ref, o_ref, h_ref, *, final_norm):
    j = pl.program_id(1)

    @pl.when(j == 0)
    def _():
        x = x_ref[...]
        h_ref[...] = _rms(x, g_ref[...]).astype(BF16)
        o_ref[...] = x

    h = h_ref[...]
    gate = _dot(h, wg_ref[...])
    up = _dot(h, wu_ref[...])
    act = (gate * (1.0 / (1.0 + jnp.exp(-gate))) * up).astype(BF16)
    o_ref[...] += _dot(act, wd_ref[...])

    if final_norm:
        @pl.when(j == pl.num_programs(1) - 1)
        def _():
            o_ref[...] = _rms(o_ref[...], gf_ref[...])


def _ffn(x, g, wg, wu, wd, gf, final_norm, tm, tf):
    t, d = x.shape
    ff = wg.shape[1]
    row = pl.BlockSpec((tm, d), lambda i, j: (i, 0))
    vec = pl.BlockSpec((1, d), lambda i, j: (0, 0))
    return pl.pallas_call(
        functools.partial(_ffn_kernel, final_norm=final_norm),
        grid=(t // tm, ff // tf),
        in_specs=[row, vec, pl.BlockSpec((d, tf), lambda i, j: (0, j)),
                  pl.BlockSpec((d, tf), lambda i, j: (0, j)),
                  pl.BlockSpec((tf, d), lambda i, j: (j, 0)), vec],
        out_specs=row,
        out_shape=jax.ShapeDtypeStruct((t, d), F32),
        scratch_shapes=[pltpu.VMEM((tm, d), BF16)],
        compiler_params=_cparams(("parallel", "arbitrary")),
        name="ffn",
    )(x, g, wg, wu, wd, gf)


def _rotate_half_cols(w):
    half = MLA_ROPE // 2
    return jnp.concatenate([-w[..., half:], w[..., :half]], axis=-1)


def _rope_tables(positions, scale):
    inv_freq = ROPE_THETA ** (-jnp.arange(0, MLA_ROPE, 2, dtype=F32) / MLA_ROPE)
    ang = positions.astype(F32).reshape(-1, 1) * inv_freq
    cos2 = jnp.concatenate([jnp.cos(ang)] * 2, axis=-1)
    sin2 = jnp.concatenate([jnp.sin(ang)] * 2, axis=-1)
    t = ang.shape[0]
    pad = jnp.zeros((t, MLA_QK_PAD - MLA_NOPE - MLA_ROPE), F32)
    cq = jnp.concatenate([jnp.ones((t, MLA_NOPE), F32), cos2, pad], axis=-1) * scale
    sq = jnp.concatenate([jnp.zeros((t, MLA_NOPE), F32), sin2, pad], axis=-1) * scale
    ck = jnp.concatenate([cos2, jnp.zeros((t, LANES - MLA_ROPE), F32)], axis=-1)
    sk = jnp.concatenate([sin2, jnp.zeros((t, LANES - MLA_ROPE), F32)], axis=-1)
    return cq, sq, ck, sk


def _mla_weights(w_in_mla, w_uq, w_ukv):
    d = w_in_mla.shape[0]
    w_cq = w_in_mla[:, :MLA_Q_RANK]
    w_ckv = w_in_mla[:, MLA_Q_RANK:MLA_Q_RANK + MLA_KV_RANK]
    w_kr = w_in_mla[:, MLA_Q_RANK + MLA_KV_RANK:]
    zpad = jnp.zeros((d, LANES - MLA_ROPE), F32)
    w_in = jnp.concatenate([w_cq, w_ckv, w_kr, zpad, _rotate_half_cols(w_kr), zpad], axis=1)

    uq = w_uq.reshape(MLA_Q_RANK, MLA_HEADS, MLA_NOPE + MLA_ROPE)
    nope, rope = uq[..., :MLA_NOPE], uq[..., MLA_NOPE:]
    zr = jnp.zeros((MLA_Q_RANK, MLA_HEADS, MLA_QK_PAD - MLA_NOPE - MLA_ROPE), F32)
    wa = jnp.concatenate([nope, rope, zr], axis=-1).reshape(MLA_Q_RANK, MLA_HEADS * MLA_QK_PAD)
    wb = jnp.concatenate([jnp.zeros_like(nope), _rotate_half_cols(rope), zr], axis=-1)
    wb = wb.reshape(MLA_Q_RANK, MLA_HEADS * MLA_QK_PAD)

    ukv = w_ukv.reshape(MLA_KV_RANK, MLA_HEADS, MLA_NOPE + MLA_V)
    wk = ukv[..., :MLA_NOPE].reshape(MLA_KV_RANK, MLA_HEADS * MLA_NOPE)
    wv = ukv[..., MLA_NOPE:].reshape(MLA_KV_RANK, MLA_HEADS * MLA_V)
    return tuple(w.astype(BF16) for w in (w_in, wa, wb, wk, wv))


def kernel(x, positions, attn_norm_g, w_in, hg_lower_bounds, hg_norm_g, sb_norm_g, mla_q_norm_g,
           mla_w_uq, mla_kv_norm_g, mla_w_ukv, mla_out_norm_g, w_o, ffn_norm_g, w_gate, w_up,
           w_down, final_norm_g):
    batch, seq, d_model = x.shape
    depth = w_in.shape[0]
    t = batch * seq
    hg_w = HG_HEADS * HG_D
    sb_w = SB_HEADS * SB_D
    mla_w = MLA_HEADS * MLA_V
    hg_cols = 4 * hg_w
    sb_cols = 3 * sb_w

    sm = jax.nn.softmax(hg_lower_bounds.astype(F32), axis=0)
    lower_bounds = jnp.cumsum(sm, axis=0) - sm[0:1]
    tables = _rope_tables(positions, (MLA_NOPE + MLA_ROPE) ** -0.5)

    xt = x.reshape(t, d_model)
    for layer in range(depth):
        g_attn = attn_norm_g[layer].reshape(1, d_model)
        w_l = w_in[layer]
        w_hg = w_l[:, :hg_cols].astype(BF16)
        w_sb = w_l[:, hg_cols:hg_cols + sb_cols].astype(BF16)
        mla_ws = _mla_weights(w_l[:, hg_cols + sb_cols:], mla_w_uq[layer], mla_w_ukv[layer])

        hg_proj = _norm_proj(xt, g_attn, w_hg, F32, tm=min(t, 1024), tn=1024)
        sb_proj = _norm_proj(xt, g_attn, w_sb, BF16, tm=min(t, 1024), tn=sb_w)
        mq, mk, mv = _mla_proj(xt, g_attn, mla_ws[0], mla_q_norm_g[layer].reshape(1, -1), mla_ws[1],
                               mla_ws[2], mla_kv_norm_g[layer].reshape(1, -1), mla_ws[3], mla_ws[4],
                               *tables, tm=min(t, 512))

        hg_out = _hgrn(hg_proj, lower_bounds[layer].reshape(1, hg_w), hg_norm_g[layer].reshape(1, hg_w),
                       batch, tb=min(seq, 512))
        sb_out = _sb_attention(sb_proj, sb_norm_g[layer].reshape(1, sb_w), batch, blk=min(seq, 256),
                               heads=ATTN_HEADS_PER_STEP)
        mla_out = _mla_attention(mq, mk, mv, mla_out_norm_g[layer].reshape(1, mla_w), batch,
                                 blk=min(seq, 256), heads=ATTN_HEADS_PER_STEP)

        wo = w_o[layer].astype(BF16)
        xt = _out_proj(xt, hg_out, sb_out, mla_out, wo[:hg_w], wo[hg_w:hg_w + sb_w], wo[hg_w + sb_w:],
                       tm=min(t, 1024), tn=1024)
        xt = _ffn(xt, ffn_norm_g[layer].reshape(1, d_model), w_gate[layer].astype(BF16),
                  w_up[layer].astype(BF16), w_down[layer].astype(BF16), final_norm_g.reshape(1, d_model),
                  final_norm=(layer == depth - 1), tm=min(t, 512), tf=512)
    return xt.reshape(batch, seq, d_model)
```

```python
import functools

import jax
import jax.numpy as jnp
from jax import lax
from jax.experimental import pallas as pl
from jax.experimental.pallas import tpu as pltpu

F32 = jnp.float32
BF16 = jnp.bfloat16

NORM_EPS = 1e-6
LOG2E = 1.4426950408889634
ROPE_THETA = 10000.0
CHUNK = 64

HG_HEADS = 4
HG_D = 128
SB_HEADS = 6
SB_D = 128
MLA_HEADS = 6
MLA_NOPE = 128
MLA_ROPE = 64
MLA_V = 128
MLA_Q_RANK = 512
MLA_KV_RANK = 256
MLA_QK_PAD = 256

ATTN_HEADS_PER_STEP = 6
HG_CHUNK = 64
HG_SUB = 16
SUBLANES = 8
LANES = 128
VMEM_LIMIT = 52 * 1024 * 1024


def _cparams(semantics):
    return pltpu.CompilerParams(dimension_semantics=semantics, vmem_limit_bytes=VMEM_LIMIT)


def _rms(x, g):
    return x * lax.rsqrt(jnp.mean(x * x, axis=-1, keepdims=True) + NORM_EPS) * g


def _dot(a, b):
    return jnp.dot(a, b, preferred_element_type=F32)


def _dot_nt(a, b):
    return lax.dot_general(a, b, (((1,), (1,)), ((), ())), preferred_element_type=F32)


def _dot_tn(a, b):
    return lax.dot_general(a, b, (((0,), (0,)), ((), ())), preferred_element_type=F32)


def _norm_proj_kernel(x_ref, g_ref, w_ref, o_ref, h_ref):
    @pl.when(pl.program_id(1) == 0)
    def _():
        h_ref[...] = _rms(x_ref[...], g_ref[...]).astype(BF16)

    o_ref[...] = _dot(h_ref[...], w_ref[...])


def _norm_proj(x, g, w, tm, tn):
    t, d = x.shape
    n = w.shape[1]
    return pl.pallas_call(
        _norm_proj_kernel,
        grid=(t // tm, n // tn),
        in_specs=[
            pl.BlockSpec((tm, d), lambda i, j: (i, 0)),
            pl.BlockSpec((1, d), lambda i, j: (0, 0)),
            pl.BlockSpec((d, tn), lambda i, j: (0, j)),
        ],
        out_specs=[pl.BlockSpec((tm, tn), lambda i, j: (i, j)), pl.BlockSpec((tm, d), lambda i, j: (i, 0))],
        out_shape=[jax.ShapeDtypeStruct((t, n), F32), jax.ShapeDtypeStruct((t, d), BF16)],
        compiler_params=_cparams(("parallel", "arbitrary")),
        name="norm_proj",
    )(x, g, w)


def _proj_kernel(h_ref, w_ref, o_ref, *, first_tile_scale):
    factor = jnp.where(pl.program_id(1) == 0, first_tile_scale, 1.0)
    o_ref[...] = (_dot(h_ref[...], w_ref[...]) * factor).astype(o_ref.dtype)


def _proj(h, w, first_tile_scale, tm, tn):
    t, d = h.shape
    n = w.shape[1]
    return pl.pallas_call(
        functools.partial(_proj_kernel, first_tile_scale=first_tile_scale),
        grid=(t // tm, n // tn),
        in_specs=[pl.BlockSpec((tm, d), lambda i, j: (i, 0)), pl.BlockSpec((d, tn), lambda i, j: (0, j))],
        out_specs=pl.BlockSpec((tm, tn), lambda i, j: (i, j)),
        out_shape=jax.ShapeDtypeStruct((t, n), BF16),
        compiler_params=_cparams(("parallel", "parallel")),
        name="proj",
    )(h, w)


def _mla_proj_kernel(h_ref, win_ref, gq_ref, wa_ref, wb_ref, gkv_ref, wk_ref, wv_ref,
                     cq_ref, sq_ref, ck_ref, sk_ref, q_ref, k_ref, v_ref):
    p = _dot(h_ref[...], win_ref[...])
    c_q = p[:, :MLA_Q_RANK]
    c_kv = p[:, MLA_Q_RANK:MLA_Q_RANK + MLA_KV_RANK]
    k_rope = p[:, MLA_Q_RANK + MLA_KV_RANK:MLA_Q_RANK + MLA_KV_RANK + LANES]
    k_rope_rot = p[:, MLA_Q_RANK + MLA_KV_RANK + LANES:]

    cqn = _rms(c_q, gq_ref[...]).astype(BF16)
    qa = _dot(cqn, wa_ref[...])
    qb = _dot(cqn, wb_ref[...])
    cq = cq_ref[...]
    sq = sq_ref[...]
    for hd in range(MLA_HEADS):
        sl = slice(hd * MLA_QK_PAD, (hd + 1) * MLA_QK_PAD)
        q_ref[:, sl] = (qa[:, sl] * cq + qb[:, sl] * sq).astype(BF16)

    ckvn = _rms(c_kv, gkv_ref[...]).astype(BF16)
    kn = _dot(ckvn, wk_ref[...])
    v_ref[...] = _dot(ckvn, wv_ref[...]).astype(BF16)
    kr = (k_rope * ck_ref[...] + k_rope_rot * sk_ref[...]).astype(BF16)
    for hd in range(MLA_HEADS):
        k_ref[:, hd * MLA_QK_PAD:hd * MLA_QK_PAD + MLA_NOPE] = (
            kn[:, hd * MLA_NOPE:(hd + 1) * MLA_NOPE].astype(BF16))
        k_ref[:, hd * MLA_QK_PAD + MLA_NOPE:(hd + 1) * MLA_QK_PAD] = kr


def _mla_proj(h, w_in, gq, wa, wb, gkv, wk, wv, cq, sq, ck, sk, tm):
    t, d = h.shape
    const = lambda i: (0, 0)
    row = lambda i: (i, 0)
    full = lambda a: pl.BlockSpec(a.shape, const)
    qk_w = MLA_HEADS * MLA_QK_PAD
    v_w = MLA_HEADS * MLA_V
    return pl.pallas_call(
        _mla_proj_kernel,
        grid=(t // tm,),
        in_specs=[
            pl.BlockSpec((tm, d), row), full(w_in), full(gq), full(wa), full(wb),
            full(gkv), full(wk), full(wv),
            pl.BlockSpec((tm, MLA_QK_PAD), row), pl.BlockSpec((tm, MLA_QK_PAD), row),
            pl.BlockSpec((tm, LANES), row), pl.BlockSpec((tm, LANES), row),
        ],
        out_specs=[
            pl.BlockSpec((tm, qk_w), row), pl.BlockSpec((tm, qk_w), row), pl.BlockSpec((tm, v_w), row),
        ],
        out_shape=[
            jax.ShapeDtypeStruct((t, qk_w), BF16), jax.ShapeDtypeStruct((t, qk_w), BF16),
            jax.ShapeDtypeStruct((t, v_w), BF16),
        ],
        compiler_params=_cparams(("parallel",)),
        name="mla_proj",
    )(h, w_in, gq, wa, wb, gkv, wk, wv, cq, sq, ck, sk)


def _hgrn_diag_scores(q, g, g_ref, k_ref):
    lane = lax.broadcasted_iota(jnp.int32, (SUBLANES, LANES), 1)
    row = lax.broadcasted_iota(jnp.int32, (SUBLANES, LANES), 0)
    row1 = lax.broadcasted_iota(jnp.int32, (SUBLANES, 1), 0)
    out = []
    for tv in range(HG_CHUNK // SUBLANES):
        t0 = tv * SUBLANES
        s0 = (t0 // HG_SUB) * HG_SUB
        qv = q[t0:t0 + SUBLANES]
        gv = g[t0:t0 + SUBLANES]
        acc = jnp.zeros((SUBLANES, LANES), F32)
        for s in range(s0, t0 + SUBLANES):
            d = gv - g_ref[s:s + 1, :]
            if s >= t0:
                d = jnp.where(row + t0 >= s, d, 0.0)
            m = qv * jnp.exp(d) * k_ref[s:s + 1, :]
            r = jnp.sum(m, axis=-1, keepdims=True)
            if s >= t0:
                r = jnp.where(row1 + t0 >= s, r, 0.0)
            acc = jnp.where(lane == s, r, acc)
        out.append(acc)
    return jnp.concatenate(out, axis=0)


def _hgrn_kernel(q_ref, f_ref, v_ref, gate_ref, lb_ref, gn_ref, o_ref, st_ref, g_sc, k_sc):
    c = HG_CHUNK
    n_sub = c // HG_SUB
    n_chunks = q_ref.shape[0] // c
    heads = range(HG_HEADS)
    cols = [slice(hd * HG_D, (hd + 1) * HG_D) for hd in heads]

    @pl.when(pl.program_id(1) == 0)
    def _():
        st_ref[...] = jnp.zeros_like(st_ref)

    tri = (lax.broadcasted_iota(jnp.int32, (c, c), 0)
           >= lax.broadcasted_iota(jnp.int32, (c, c), 1)).astype(BF16)
    zeros_sub = jnp.zeros((HG_SUB, LANES), F32)

    def chunk(ci, _):
        rows = pl.ds(pl.multiple_of(ci * c, c), c)

        ks, pieces = [], []
        for cs in cols:
            fp = f_ref[rows, cs]
            lb = lb_ref[:, cs]
            e = jnp.exp(-jnp.abs(fp))
            r = 1.0 / (1.0 + e)
            sig = jnp.where(fp >= 0, r, e * r)
            nsig = jnp.where(fp >= 0, e * r, r)
            log_f = jnp.log(lb + (1.0 - lb) * sig)
            ks.append((1.0 - lb) * nsig)
            p1 = log_f.astype(BF16)
            r1 = log_f - p1.astype(F32)
            p2 = r1.astype(BF16)
            p3 = (r1 - p2.astype(F32)).astype(BF16)
            pieces.append(jnp.concatenate([p1, p2, p3], axis=1))
        gs = []
        for p in pieces:
            r = _dot(tri, p)
            gs.append(r[:, :HG_D] + r[:, HG_D:2 * HG_D] + r[:, 2 * HG_D:])

        qs, vs, q_in, k_out, q_hat, k_hat = [], [], [], [], [], []
        for hd, cs in enumerate(cols):
            g, k = gs[hd], ks[hd]
            g_sc[hd] = g
            k_sc[hd] = k
            q = q_ref[rows, cs]
            qs.append(q)
            vs.append(v_ref[rows, cs].astype(BF16))
            g_last = g_sc[hd, c - 1:c, :]
            q_in.append((q * jnp.exp(g)).astype(BF16))
            k_out.append((k * jnp.exp(g_last - g)).astype(BF16))
            qh, kh = [], []
            for sb in range(1, n_sub):
                n = sb * HG_SUB
                g_prev = g_sc[hd, n - 1:n, :]
                qh.append((q[n:n + HG_SUB] * jnp.exp(g[n:n + HG_SUB] - g_prev)).astype(BF16))
                kh.append(jnp.concatenate(
                    [k[:n] * jnp.exp(g_prev - g[:n])] + [zeros_sub] * (n_sub - sb), axis=0).astype(BF16))
            q_hat.append(qh)
            k_hat.append(kh)

        off, inter = [], []
        for hd in heads:
            off.append(jnp.concatenate(
                [zeros_sub[:, :c]] + [_dot_nt(qh, kh) for qh, kh in zip(q_hat[hd], k_hat[hd])], axis=0))
            st = st_ref[hd]
            inter.append(_dot_nt(q_in[hd], st.astype(BF16)))
            g_last = g_sc[hd, c - 1:c, :]
            st_ref[hd] = st * jnp.exp(g_last) + _dot_tn(vs[hd], k_out[hd])

        diag = [_hgrn_diag_scores(qs[hd], gs[hd], g_sc.at[hd], k_sc.at[hd]) for hd in heads]
        outs = [_dot((diag[hd][:, :c] + off[hd]).astype(BF16), vs[hd]) + inter[hd] for hd in heads]

        for hd, cs in enumerate(cols):
            y = _rms(outs[hd], gn_ref[:, cs])
            gate = gate_ref[rows, cs]
            o_ref[rows, cs] = (y * gate * (1.0 / (1.0 + jnp.exp(-gate)))).astype(o_ref.dtype)
        return 0

    lax.fori_loop(0, n_chunks, chunk, 0)


def _hgrn(proj, lb, gn, batch, tb):
    t = proj.shape[0]
    nt = t // batch // tb
    width = HG_HEADS * HG_D
    col = lambda off: pl.BlockSpec((tb, width), lambda b, i: (b * nt + i, off))
    vec = pl.BlockSpec((1, width), lambda b, i: (0, 0))
    per_head = lambda rows: pltpu.VMEM((HG_HEADS, rows, HG_D), F32)
    return pl.pallas_call(
        _hgrn_kernel,
        grid=(batch, nt),
        in_specs=[col(0), col(1), col(2), col(3), vec, vec],
        out_specs=pl.BlockSpec((tb, width), lambda b, i: (b * nt + i, 0)),
        out_shape=jax.ShapeDtypeStruct((t, width), BF16),
        scratch_shapes=[per_head(HG_D), per_head(HG_CHUNK), per_head(HG_CHUNK)],
        compiler_params=_cparams(("parallel", "arbitrary")),
        name="hgrn2",
    )(proj, proj, proj, proj, lb, gn)


def _sb_kernel(q_ref, k_ref, v_ref, u_ref, gn_ref, o_ref, *, blk, heads):
    i = pl.program_id(2)
    u2 = u_ref[...]
    row = lax.broadcasted_iota(jnp.int32, (blk, blk), 0)
    col = lax.broadcasted_iota(jnp.int32, (blk, blk), 1)
    strict = col < row

    cols = [slice(hd * SB_D, (hd + 1) * SB_D) for hd in range(heads)]

    def blocks(kb, carry, mask):
        rows = pl.ds(pl.multiple_of(kb * blk, blk), blk)
        nzs = [_dot_nt(q_ref[:, c], k_ref[rows, c]) for c in cols]
        stage = []
        for nz in nzs:
            e = jnp.exp2(jnp.abs(nz) * -LOG2E)
            log_keep = jnp.minimum(nz, 0.0) - jnp.log(1.0 + e)
            if mask is not None:
                log_keep = jnp.where(mask, log_keep, 0.0)
            hi = log_keep.astype(BF16)
            lo = (log_keep - hi.astype(F32)).astype(BF16)
            stage.append((jnp.concatenate([hi, lo], axis=1), jnp.sum(log_keep, axis=-1, keepdims=True)))
        withins = [_dot(hl, u2) for hl, _ in stage]
        weights = []
        for nz, within, (later, _) in zip(nzs, withins, carry):
            a = jnp.exp(within + later - nz)
            if mask is not None:
                a = jnp.where(mask, a, 0.0)
            weights.append(a.astype(BF16))
        return tuple((later + st[1], acc + _dot(a, v_ref[rows, c]))
                     for st, a, c, (later, acc) in zip(stage, weights, cols, carry))

    carry = tuple((jnp.zeros((blk, 1), F32), jnp.zeros((blk, SB_D), F32)) for _ in range(heads))
    carry = blocks(i, carry, strict)
    carry = lax.fori_loop(0, i, lambda j, c: blocks(i - 1 - j, c, None), carry)
    for hd, c in enumerate(cols):
        o_ref[:, c] = _rms(carry[hd][1], gn_ref[:, c]).astype(o_ref.dtype)


def _sb_attention(qkv, gn, batch, blk, heads):
    t = qkv.shape[0]
    seq = t // batch
    nq = seq // blk
    groups = SB_HEADS // heads
    width = heads * SB_D
    u = (lax.broadcasted_iota(jnp.int32, (blk, blk), 0)
         >= lax.broadcasted_iota(jnp.int32, (blk, blk), 1)).astype(BF16)
    u2 = jnp.concatenate([u, u], axis=0)
    q_spec = pl.BlockSpec((blk, width), lambda b, h, i: (b * nq + i, h))
    k_spec = pl.BlockSpec((seq, width), lambda b, h, i: (b, groups + h))
    v_spec = pl.BlockSpec((seq, width), lambda b, h, i: (b, 2 * groups + h))
    return pl.pallas_call(
        functools.partial(_sb_kernel, blk=blk, heads=heads),
        grid=(batch, groups, nq),
        in_specs=[q_spec, k_spec, v_spec, pl.BlockSpec((2 * blk, blk), lambda b, h, i: (0, 0)),
                  pl.BlockSpec((1, width), lambda b, h, i: (0, h))],
        out_specs=q_spec,
        out_shape=jax.ShapeDtypeStruct((t, SB_HEADS * SB_D), BF16),
        compiler_params=_cparams(("parallel", "parallel", "arbitrary")),
        name="sb_attention",
    )(qkv, qkv, qkv, u2, gn)


def _mla_kernel(q_ref, k_ref, v_ref, gn_ref, o_ref, *, blk, heads):
    i = pl.program_id(2)
    row = lax.broadcasted_iota(jnp.int32, (blk, blk), 0)
    col = lax.broadcasted_iota(jnp.int32, (blk, blk), 1)
    allowed = (col // CHUNK) <= (row // CHUNK)

    def blocks(kb, carry, mask):
        rows = pl.ds(pl.multiple_of(kb * blk, blk), blk)
        scores = [_dot_nt(q_ref[:, hd * MLA_QK_PAD:(hd + 1) * MLA_QK_PAD],
                          k_ref[rows, hd * MLA_QK_PAD:(hd + 1) * MLA_QK_PAD]) for hd in range(heads)]
        stage = []
        for s, (m, l, _) in zip(scores, carry):
            if mask is not None:
                s = jnp.where(mask, s, -jnp.inf)
            m_new = jnp.maximum(m, jnp.max(s, axis=-1, keepdims=True))
            alpha = jnp.exp2(m - m_new)
            p = jnp.exp2(s - m_new)
            stage.append((m_new, alpha, alpha * l + jnp.sum(p, axis=-1, keepdims=True), p.astype(BF16)))
        return tuple((m_new, l_new, alpha * acc + _dot(p, v_ref[rows, hd * MLA_V:(hd + 1) * MLA_V]))
                     for hd, ((m_new, alpha, l_new, p), (_, _, acc)) in enumerate(zip(stage, carry)))

    carry = tuple((jnp.full((blk, 1), -jnp.inf, F32), jnp.zeros((blk, 1), F32),
                   jnp.zeros((blk, MLA_V), F32)) for _ in range(heads))
    carry = blocks(i, carry, allowed)
    carry = lax.fori_loop(0, i, lambda j, c: blocks(j, c, None), carry)
    for hd in range(heads):
        cols = slice(hd * MLA_V, (hd + 1) * MLA_V)
        _, l, acc = carry[hd]
        o_ref[:, cols] = _rms(acc / l, gn_ref[:, cols]).astype(o_ref.dtype)


def _mla_attention(q, k, v, gn, batch, blk, heads):
    t = q.shape[0]
    seq = t // batch
    nq = seq // blk
    qk_w = heads * MLA_QK_PAD
    v_w = heads * MLA_V
    return pl.pallas_call(
        functools.partial(_mla_kernel, blk=blk, heads=heads),
        grid=(batch, MLA_HEADS // heads, nq),
        in_specs=[
            pl.BlockSpec((blk, qk_w), lambda b, h, i: (b * nq + i, h)),
            pl.BlockSpec((seq, qk_w), lambda b, h, i: (b, h)),
            pl.BlockSpec((seq, v_w), lambda b, h, i: (b, h)),
            pl.BlockSpec((1, v_w), lambda b, h, i: (0, h)),
        ],
        out_specs=pl.BlockSpec((blk, v_w), lambda b, h, i: (b * nq + i, h)),
        out_shape=jax.ShapeDtypeStruct((t, MLA_HEADS * MLA_V), BF16),
        compiler_params=_cparams(("parallel", "parallel", "arbitrary")),
        name="mla_attention",
    )(q, k, v, gn)


def _out_proj_kernel(x_ref, a_ref, b_ref, c_ref, w_ref, o_ref):
    ka, kb = a_ref.shape[1], b_ref.shape[1]
    o_ref[...] = (x_ref[...] + _dot(a_ref[...], w_ref[:ka]) + _dot(b_ref[...], w_ref[ka:ka + kb])
                  + _dot(c_ref[...], w_ref[ka + kb:]))


def _out_proj(x, a, b, c, w, layer, tm, tn):
    t, d = x.shape
    act = lambda arr: pl.BlockSpec((tm, arr.shape[1]), lambda i, j: (i, 0))
    tile = pl.BlockSpec((tm, tn), lambda i, j: (i, j))
    return pl.pallas_call(
        _out_proj_kernel,
        grid=(t // tm, d // tn),
        in_specs=[tile, act(a), act(b), act(c),
                  pl.BlockSpec((None, w.shape[1], tn), lambda i, j: (layer, 0, j))],
        out_specs=tile,
        out_shape=jax.ShapeDtypeStruct((t, d), F32),
        compiler_params=_cparams(("parallel", "parallel")),
        name="out_proj",
    )(x, a, b, c, w)


def _ffn_kernel(x_ref, g_ref, wg_ref, wu_ref, wd_ref, gf_ref, o_ref, h_ref, *, final_norm):
    j = pl.program_id(1)

    @pl.when(j == 0)
    def _():
        x = x_ref[...]
        h_ref[...] = _rms(x, g_ref[...]).astype(BF16)
        o_ref[...] = x

    h = h_ref[...]
    gate = _dot(h, wg_ref[...])
    up = _dot(h, wu_ref[...])
    act = (gate * (1.0 / (1.0 + jnp.exp(-gate))) * up).astype(BF16)
    o_ref[...] += _dot(act, wd_ref[...])

    if final_norm:
        @pl.when(j == pl.num_programs(1) - 1)
        def _():
            o_ref[...] = _rms(o_ref[...], gf_ref[...])


def _ffn(x, g, wg, wu, wd, gf, layer, final_norm, tm, tf):
    t, d = x.shape
    ff = wg.shape[2]
    row = pl.BlockSpec((tm, d), lambda i, j: (i, 0))
    vec = pl.BlockSpec((1, d), lambda i, j: (0, 0))
    return pl.pallas_call(
        functools.partial(_ffn_kernel, final_norm=final_norm),
        grid=(t // tm, ff // tf),
        in_specs=[row, vec, pl.BlockSpec((None, d, tf), lambda i, j: (layer, 0, j)),
                  pl.BlockSpec((None, d, tf), lambda i, j: (layer, 0, j)),
                  pl.BlockSpec((None, tf, d), lambda i, j: (layer, j, 0)), vec],
        out_specs=row,
        out_shape=jax.ShapeDtypeStruct((t, d), F32),
        scratch_shapes=[pltpu.VMEM((tm, d), BF16)],
        compiler_params=_cparams(("parallel", "arbitrary")),
        name="ffn",
    )(x, g, wg, wu, wd, gf)


def _rotate_half_cols(w):
    half = MLA_ROPE // 2
    return jnp.concatenate([-w[..., half:], w[..., :half]], axis=-1)


def _rope_tables(positions, scale):
    inv_freq = ROPE_THETA ** (-jnp.arange(0, MLA_ROPE, 2, dtype=F32) / MLA_ROPE)
    ang = positions.astype(F32).reshape(-1, 1) * inv_freq
    cos2 = jnp.concatenate([jnp.cos(ang)] * 2, axis=-1)
    sin2 = jnp.concatenate([jnp.sin(ang)] * 2, axis=-1)
    t = ang.shape[0]
    pad = jnp.zeros((t, MLA_QK_PAD - MLA_NOPE - MLA_ROPE), F32)
    cq = jnp.concatenate([jnp.ones((t, MLA_NOPE), F32), cos2, pad], axis=-1) * scale
    sq = jnp.concatenate([jnp.zeros((t, MLA_NOPE), F32), sin2, pad], axis=-1) * scale
    ck = jnp.concatenate([cos2, jnp.zeros((t, LANES - MLA_ROPE), F32)], axis=-1)
    sk = jnp.concatenate([sin2, jnp.zeros((t, LANES - MLA_ROPE), F32)], axis=-1)
    return cq, sq, ck, sk


def _mla_weights(w_in_mla, w_uq, w_ukv):
    d = w_in_mla.shape[0]
    w_cq = w_in_mla[:, :MLA_Q_RANK]
    w_ckv = w_in_mla[:, MLA_Q_RANK:MLA_Q_RANK + MLA_KV_RANK]
    w_kr = w_in_mla[:, MLA_Q_RANK + MLA_KV_RANK:]
    zpad = jnp.zeros((d, LANES - MLA_ROPE), F32)
    w_in = jnp.concatenate([w_cq, w_ckv, w_kr, zpad, _rotate_half_cols(w_kr), zpad], axis=1)

    uq = w_uq.reshape(MLA_Q_RANK, MLA_HEADS, MLA_NOPE + MLA_ROPE)
    nope, rope = uq[..., :MLA_NOPE], uq[..., MLA_NOPE:]
    zr = jnp.zeros((MLA_Q_RANK, MLA_HEADS, MLA_QK_PAD - MLA_NOPE - MLA_ROPE), F32)
    wa = jnp.concatenate([nope, rope, zr], axis=-1).reshape(MLA_Q_RANK, MLA_HEADS * MLA_QK_PAD)
    wb = jnp.concatenate([jnp.zeros_like(nope), _rotate_half_cols(rope), zr], axis=-1)
    wb = wb.reshape(MLA_Q_RANK, MLA_HEADS * MLA_QK_PAD)

    ukv = w_ukv.reshape(MLA_KV_RANK, MLA_HEADS, MLA_NOPE + MLA_V)
    wk = ukv[..., :MLA_NOPE].reshape(MLA_KV_RANK, MLA_HEADS * MLA_NOPE)
    wv = ukv[..., MLA_NOPE:].reshape(MLA_KV_RANK, MLA_HEADS * MLA_V)
    return tuple(w.astype(BF16) for w in (w_in, wa, wb, wk, wv))


def kernel(x, positions, attn_norm_g, w_in, hg_lower_bounds, hg_norm_g, sb_norm_g, mla_q_norm_g,
           mla_w_uq, mla_kv_norm_g, mla_w_ukv, mla_out_norm_g, w_o, ffn_norm_g, w_gate, w_up,
           w_down, final_norm_g):
    batch, seq, d_model = x.shape
    depth = w_in.shape[0]
    t = batch * seq
    hg_w = HG_HEADS * HG_D
    sb_w = SB_HEADS * SB_D
    mla_w = MLA_HEADS * MLA_V
    hg_cols = 4 * hg_w
    sb_cols = 3 * sb_w

    sm = jax.nn.softmax(hg_lower_bounds.astype(F32), axis=0)
    lower_bounds = jnp.cumsum(sm, axis=0) - sm[0:1]
    tables = _rope_tables(positions, (MLA_NOPE + MLA_ROPE) ** -0.5 * LOG2E)

    wo, wg, wu, wd = (w.astype(BF16) for w in (w_o, w_gate, w_up, w_down))

    xt = x.reshape(t, d_model)
    for layer in range(depth):
        g_attn = attn_norm_g[layer].reshape(1, d_model)
        w_l = w_in[layer]
        w_hg = w_l[:, :hg_cols].astype(BF16)
        w_sb = w_l[:, hg_cols:hg_cols + sb_cols].astype(BF16)
        mla_ws = _mla_weights(w_l[:, hg_cols + sb_cols:], mla_w_uq[layer], mla_w_ukv[layer])

        hg_proj, h = _norm_proj(xt, g_attn, w_hg, tm=min(t, 1024), tn=1024)
        sb_proj = _proj(h, w_sb, -(SB_D ** -0.5), tm=min(t, 1024), tn=sb_w)
        mq, mk, mv = _mla_proj(h, mla_ws[0], mla_q_norm_g[layer].reshape(1, -1), mla_ws[1],
                               mla_ws[2], mla_kv_norm_g[layer].reshape(1, -1), mla_ws[3], mla_ws[4],
                               *tables, tm=min(t, 512))

        hg_out = _hgrn(hg_proj, lower_bounds[layer].reshape(1, hg_w), hg_norm_g[layer].reshape(1, hg_w),
                       batch, tb=min(seq, 512))
        sb_out = _sb_attention(sb_proj, sb_norm_g[layer].reshape(1, sb_w), batch, blk=min(seq, 256),
                               heads=ATTN_HEADS_PER_STEP)
        mla_out = _mla_attention(mq, mk, mv, mla_out_norm_g[layer].reshape(1, mla_w), batch,
                                 blk=min(seq, 256), heads=ATTN_HEADS_PER_STEP)

        xt = _out_proj(xt, hg_out, sb_out, mla_out, wo, layer, tm=min(t, 1024), tn=1024)
        xt = _ffn(xt, ffn_norm_g[layer].reshape(1, d_model), wg, wu, wd, final_norm_g.reshape(1, d_model),
                  layer, final_norm=(layer == depth - 1), tm=min(t, 512), tf=512)
    return xt.reshape(batch, seq, d_model)
```

```python
import functools

import jax
import jax.numpy as jnp
from jax import lax
from jax.experimental import pallas as pl
from jax.experimental.pallas import tpu as pltpu

F32 = jnp.float32
BF16 = jnp.bfloat16

NORM_EPS = 1e-6
LOG2E = 1.4426950408889634
ROPE_THETA = 10000.0
CHUNK = 64

HG_HEADS = 4
HG_D = 128
SB_HEADS = 6
SB_D = 128
MLA_HEADS = 6
MLA_NOPE = 128
MLA_ROPE = 64
MLA_V = 128
MLA_Q_RANK = 512
MLA_KV_RANK = 256
MLA_QK_PAD = 256

ATTN_HEADS_PER_STEP = 6
HG_CHUNK = 64
HG_SUB = 16
SUBLANES = 8
LANES = 128
VMEM_LIMIT = 52 * 1024 * 1024


def _cparams(semantics):
    return pltpu.CompilerParams(dimension_semantics=semantics, vmem_limit_bytes=VMEM_LIMIT)


def _rms(x, g):
    return x * lax.rsqrt(jnp.mean(x * x, axis=-1, keepdims=True) + NORM_EPS) * g


def _dot(a, b):
    return jnp.dot(a, b, preferred_element_type=F32)


def _dot_nt(a, b):
    return lax.dot_general(a, b, (((1,), (1,)), ((), ())), preferred_element_type=F32)


def _dot_tn(a, b):
    return lax.dot_general(a, b, (((0,), (0,)), ((), ())), preferred_element_type=F32)


def _norm_proj_kernel(x_ref, g_ref, w_ref, o_ref, h_ref):
    @pl.when(pl.program_id(1) == 0)
    def _():
        h_ref[...] = _rms(x_ref[...], g_ref[...]).astype(BF16)

    o_ref[...] = _dot(h_ref[...], w_ref[...])


def _norm_proj(x, g, w, tm, tn):
    t, d = x.shape
    n = w.shape[1]
    return pl.pallas_call(
        _norm_proj_kernel,
        grid=(t // tm, n // tn),
        in_specs=[
            pl.BlockSpec((tm, d), lambda i, j: (i, 0)),
            pl.BlockSpec((1, d), lambda i, j: (0, 0)),
            pl.BlockSpec((d, tn), lambda i, j: (0, j)),
        ],
        out_specs=[pl.BlockSpec((tm, tn), lambda i, j: (i, j)), pl.BlockSpec((tm, d), lambda i, j: (i, 0))],
        out_shape=[jax.ShapeDtypeStruct((t, n), F32), jax.ShapeDtypeStruct((t, d), BF16)],
        compiler_params=_cparams(("parallel", "arbitrary")),
        name="norm_proj",
    )(x, g, w)


def _proj_kernel(h_ref, w_ref, o_ref, *, first_tile_scale):
    factor = jnp.where(pl.program_id(1) == 0, first_tile_scale, 1.0)
    o_ref[...] = (_dot(h_ref[...], w_ref[...]) * factor).astype(o_ref.dtype)


def _proj(h, w, first_tile_scale, tm, tn):
    t, d = h.shape
    n = w.shape[1]
    return pl.pallas_call(
        functools.partial(_proj_kernel, first_tile_scale=first_tile_scale),
        grid=(t // tm, n // tn),
        in_specs=[pl.BlockSpec((tm, d), lambda i, j: (i, 0)), pl.BlockSpec((d, tn), lambda i, j: (0, j))],
        out_specs=pl.BlockSpec((tm, tn), lambda i, j: (i, j)),
        out_shape=jax.ShapeDtypeStruct((t, n), BF16),
        compiler_params=_cparams(("parallel", "parallel")),
        name="proj",
    )(h, w)


def _mla_proj_kernel(h_ref, win_ref, gq_ref, wa_ref, wb_ref, gkv_ref, wk_ref, wv_ref,
                     cq_ref, sq_ref, ck_ref, sk_ref, q_ref, k_ref, v_ref):
    p = _dot(h_ref[...], win_ref[...])
    c_q = p[:, :MLA_Q_RANK]
    c_kv = p[:, MLA_Q_RANK:MLA_Q_RANK + MLA_KV_RANK]
    k_rope = p[:, MLA_Q_RANK + MLA_KV_RANK:MLA_Q_RANK + MLA_KV_RANK + LANES]
    k_rope_rot = p[:, MLA_Q_RANK + MLA_KV_RANK + LANES:]

    cqn = _rms(c_q, gq_ref[...]).astype(BF16)
    qa = _dot(cqn, wa_ref[...])
    qb = _dot(cqn, wb_ref[...])
    cq = cq_ref[...]
    sq = sq_ref[...]
    for hd in range(MLA_HEADS):
        sl = slice(hd * MLA_QK_PAD, (hd + 1) * MLA_QK_PAD)
        q_ref[:, sl] = (qa[:, sl] * cq + qb[:, sl] * sq).astype(BF16)

    ckvn = _rms(c_kv, gkv_ref[...]).astype(BF16)
    kn = _dot(ckvn, wk_ref[...])
    v_ref[...] = _dot(ckvn, wv_ref[...]).astype(BF16)
    kr = (k_rope * ck_ref[...] + k_rope_rot * sk_ref[...]).astype(BF16)
    for hd in range(MLA_HEADS):
        k_ref[:, hd * MLA_QK_PAD:hd * MLA_QK_PAD + MLA_NOPE] = (
            kn[:, hd * MLA_NOPE:(hd + 1) * MLA_NOPE].astype(BF16))
        k_ref[:, hd * MLA_QK_PAD + MLA_NOPE:(hd + 1) * MLA_QK_PAD] = kr


def _mla_proj(h, w_in, gq, wa, wb, gkv, wk, wv, cq, sq, ck, sk, tm):
    t, d = h.shape
    const = lambda i: (0, 0)
    row = lambda i: (i, 0)
    full = lambda a: pl.BlockSpec(a.shape, const)
    qk_w = MLA_HEADS * MLA_QK_PAD
    v_w = MLA_HEADS * MLA_V
    return pl.pallas_call(
        _mla_proj_kernel,
        grid=(t // tm,),
        in_specs=[
            pl.BlockSpec((tm, d), row), full(w_in), full(gq), full(wa), full(wb),
            full(gkv), full(wk), full(wv),
            pl.BlockSpec((tm, MLA_QK_PAD), row), pl.BlockSpec((tm, MLA_QK_PAD), row),
            pl.BlockSpec((tm, LANES), row), pl.BlockSpec((tm, LANES), row),
        ],
        out_specs=[
            pl.BlockSpec((tm, qk_w), row), pl.BlockSpec((tm, qk_w), row), pl.BlockSpec((tm, v_w), row),
        ],
        out_shape=[
            jax.ShapeDtypeStruct((t, qk_w), BF16), jax.ShapeDtypeStruct((t, qk_w), BF16),
            jax.ShapeDtypeStruct((t, v_w), BF16),
        ],
        compiler_params=_cparams(("parallel",)),
        name="mla_proj",
    )(h, w_in, gq, wa, wb, gkv, wk, wv, cq, sq, ck, sk)


def _hgrn_diag_scores(q, g2, h_ref):
    lane = lax.broadcasted_iota(jnp.int32, (SUBLANES, LANES), 1)
    row1 = lax.broadcasted_iota(jnp.int32, (SUBLANES, 1), 0)
    out = []
    for tv in range(HG_CHUNK // SUBLANES):
        t0 = tv * SUBLANES
        s0 = (t0 // HG_SUB) * HG_SUB
        qv = q[t0:t0 + SUBLANES]
        gv = g2[t0:t0 + SUBLANES]
        acc = jnp.zeros((SUBLANES, LANES), F32)
        for s in range(s0, t0 + SUBLANES):
            m = qv * jnp.exp2(gv - h_ref[s:s + 1, :])
            r = jnp.sum(m, axis=-1, keepdims=True)
            if s >= t0:
                r = jnp.where(row1 + t0 >= s, r, 0.0)
            acc = jnp.where(lane == s, r, acc)
        out.append(acc)
    return jnp.concatenate(out, axis=0)


def _hgrn_kernel(q_ref, f_ref, v_ref, gate_ref, lb_ref, gn_ref, o_ref, st_ref, g_sc, h_sc):
    c = HG_CHUNK
    n_sub = c // HG_SUB
    n_chunks = q_ref.shape[0] // c
    heads = range(HG_HEADS)
    cols = [slice(hd * HG_D, (hd + 1) * HG_D) for hd in heads]

    @pl.when(pl.program_id(1) == 0)
    def _():
        st_ref[...] = jnp.zeros_like(st_ref)

    tri = (lax.broadcasted_iota(jnp.int32, (c, c), 0)
           >= lax.broadcasted_iota(jnp.int32, (c, c), 1)).astype(BF16)
    zeros_sub = jnp.zeros((HG_SUB, LANES), F32)

    def chunk(ci, _):
        rows = pl.ds(pl.multiple_of(ci * c, c), c)

        ks, pieces = [], []
        for cs in cols:
            fp = f_ref[rows, cs]
            lb = lb_ref[:, cs]
            e = jnp.exp(-jnp.abs(fp))
            r = 1.0 / (1.0 + e)
            sig = jnp.where(fp >= 0, r, e * r)
            nsig = jnp.where(fp >= 0, e * r, r)
            log_f = jnp.log(lb + (1.0 - lb) * sig)
            ks.append((1.0 - lb) * nsig)
            p1 = log_f.astype(BF16)
            r1 = log_f - p1.astype(F32)
            p2 = r1.astype(BF16)
            p3 = (r1 - p2.astype(F32)).astype(BF16)
            pieces.append(jnp.concatenate([p1, p2, p3], axis=1))
        gs = []
        for p in pieces:
            r = _dot(tri, p)
            gs.append(r[:, :HG_D] + r[:, HG_D:2 * HG_D] + r[:, 2 * HG_D:])

        qs, vs, g2s, q_in, k_out, q_hat, k_hat = [], [], [], [], [], [], []
        for hd, cs in enumerate(cols):
            g, k = gs[hd], ks[hd]
            g2 = g * LOG2E
            g_sc[hd] = g
            h_sc[hd] = g2 - jnp.log(k) * LOG2E
            g2s.append(g2)
            q = q_ref[rows, cs]
            qs.append(q)
            vs.append(v_ref[rows, cs].astype(BF16))
            g_last = g_sc[hd, c - 1:c, :]
            q_in.append((q * jnp.exp(g)).astype(BF16))
            k_out.append((k * jnp.exp(g_last - g)).astype(BF16))
            qh, kh = [], []
            for sb in range(1, n_sub):
                n = sb * HG_SUB
                g_prev = g_sc[hd, n - 1:n, :]
                qh.append((q[n:n + HG_SUB] * jnp.exp(g[n:n + HG_SUB] - g_prev)).astype(BF16))
                kh.append(jnp.concatenate(
                    [k[:n] * jnp.exp(g_prev - g[:n])] + [zeros_sub] * (n_sub - sb), axis=0).astype(BF16))
            q_hat.append(qh)
            k_hat.append(kh)

        off, inter = [], []
        for hd in heads:
            off.append(jnp.concatenate(
                [zeros_sub[:, :c]] + [_dot_nt(qh, kh) for qh, kh in zip(q_hat[hd], k_hat[hd])], axis=0))
            st = st_ref[hd]
            inter.append(_dot_nt(q_in[hd], st.astype(BF16)))
            g_last = g_sc[hd, c - 1:c, :]
            st_ref[hd] = st * jnp.exp(g_last) + _dot_tn(vs[hd], k_out[hd])

        diag = [_hgrn_diag_scores(qs[hd], g2s[hd], h_sc.at[hd]) for hd in heads]
        outs = [_dot((diag[hd][:, :c] + off[hd]).astype(BF16), vs[hd]) + inter[hd] for hd in heads]

        for hd, cs in enumerate(cols):
            y = _rms(outs[hd], gn_ref[:, cs])
            gate = gate_ref[rows, cs]
            o_ref[rows, cs] = (y * gate * (1.0 / (1.0 + jnp.exp(-gate)))).astype(o_ref.dtype)
        return 0

    lax.fori_loop(0, n_chunks, chunk, 0)


def _hgrn(proj, lb, gn, batch, tb):
    t = proj.shape[0]
    nt = t // batch // tb
    width = HG_HEADS * HG_D
    col = lambda off: pl.BlockSpec((tb, width), lambda b, i: (b * nt + i, off))
    vec = pl.BlockSpec((1, width), lambda b, i: (0, 0))
    per_head = lambda rows: pltpu.VMEM((HG_HEADS, rows, HG_D), F32)
    return pl.pallas_call(
        _hgrn_kernel,
        grid=(batch, nt),
        in_specs=[col(0), col(1), col(2), col(3), vec, vec],
        out_specs=pl.BlockSpec((tb, width), lambda b, i: (b * nt + i, 0)),
        out_shape=jax.ShapeDtypeStruct((t, width), BF16),
        scratch_shapes=[per_head(HG_D), per_head(HG_CHUNK), per_head(HG_CHUNK)],
        compiler_params=_cparams(("parallel", "arbitrary")),
        name="hgrn2",
    )(proj, proj, proj, proj, lb, gn)


def _sb_kernel(q_ref, k_ref, v_ref, u_ref, gn_ref, o_ref, *, blk, heads):
    i = pl.program_id(2)
    u = u_ref[...]
    row = lax.broadcasted_iota(jnp.int32, (blk, blk), 0)
    col = lax.broadcasted_iota(jnp.int32, (blk, blk), 1)
    strict = col < row

    cols = [slice(hd * SB_D, (hd + 1) * SB_D) for hd in range(heads)]

    def blocks(kb, carry, mask):
        rows = pl.ds(pl.multiple_of(kb * blk, blk), blk)
        nzs = [_dot_nt(q_ref[:, c], k_ref[rows, c]) for c in cols]
        stage = []
        for nz in nzs:
            e = jnp.exp2(jnp.abs(nz) * -LOG2E)
            log_keep = jnp.minimum(nz, 0.0) - jnp.log(1.0 + e)
            if mask is not None:
                log_keep = jnp.where(mask, log_keep, 0.0)
            stage.append((log_keep.astype(BF16), jnp.sum(log_keep, axis=-1, keepdims=True)))
        withins = [_dot(lk, u) for lk, _ in stage]
        weights = []
        for nz, within, (later, _) in zip(nzs, withins, carry):
            a = jnp.exp(within + later - nz)
            if mask is not None:
                a = jnp.where(mask, a, 0.0)
            weights.append(a.astype(BF16))
        return tuple((later + st[1], acc + _dot(a, v_ref[rows, c]))
                     for st, a, c, (later, acc) in zip(stage, weights, cols, carry))

    carry = tuple((jnp.zeros((blk, 1), F32), jnp.zeros((blk, SB_D), F32)) for _ in range(heads))
    carry = blocks(i, carry, strict)
    carry = lax.fori_loop(0, i, lambda j, c: blocks(i - 1 - j, c, None), carry)
    for hd, c in enumerate(cols):
        o_ref[:, c] = _rms(carry[hd][1], gn_ref[:, c]).astype(o_ref.dtype)


def _sb_attention(qkv, gn, batch, blk, heads):
    t = qkv.shape[0]
    seq = t // batch
    nq = seq // blk
    groups = SB_HEADS // heads
    width = heads * SB_D
    u = (lax.broadcasted_iota(jnp.int32, (blk, blk), 0)
         >= lax.broadcasted_iota(jnp.int32, (blk, blk), 1)).astype(BF16)
    q_spec = pl.BlockSpec((blk, width), lambda b, h, i: (b * nq + i, h))
    k_spec = pl.BlockSpec((seq, width), lambda b, h, i: (b, groups + h))
    v_spec = pl.BlockSpec((seq, width), lambda b, h, i: (b, 2 * groups + h))
    return pl.pallas_call(
        functools.partial(_sb_kernel, blk=blk, heads=heads),
        grid=(batch, groups, nq),
        in_specs=[q_spec, k_spec, v_spec, pl.BlockSpec((blk, blk), lambda b, h, i: (0, 0)),
                  pl.BlockSpec((1, width), lambda b, h, i: (0, h))],
        out_specs=q_spec,
        out_shape=jax.ShapeDtypeStruct((t, SB_HEADS * SB_D), BF16),
        compiler_params=_cparams(("parallel", "parallel", "arbitrary")),
        name="sb_attention",
    )(qkv, qkv, qkv, u, gn)


def _mla_kernel(q_ref, k_ref, v_ref, gn_ref, o_ref, *, blk, heads):
    i = pl.program_id(2)
    row = lax.broadcasted_iota(jnp.int32, (blk, blk), 0)
    col = lax.broadcasted_iota(jnp.int32, (blk, blk), 1)
    allowed = (col // CHUNK) <= (row // CHUNK)

    def blocks(kb, carry, mask):
        rows = pl.ds(pl.multiple_of(kb * blk, blk), blk)
        scores = [_dot_nt(q_ref[:, hd * MLA_QK_PAD:(hd + 1) * MLA_QK_PAD],
                          k_ref[rows, hd * MLA_QK_PAD:(hd + 1) * MLA_QK_PAD]) for hd in range(heads)]
        stage = []
        for s, (m, l, _) in zip(scores, carry):
            if mask is not None:
                s = jnp.where(mask, s, -jnp.inf)
            m_new = jnp.maximum(m, jnp.max(s, axis=-1, keepdims=True))
            alpha = jnp.exp2(m - m_new)
            p = jnp.exp2(s - m_new)
            stage.append((m_new, alpha, alpha * l + jnp.sum(p, axis=-1, keepdims=True), p.astype(BF16)))
        return tuple((m_new, l_new, alpha * acc + _dot(p, v_ref[rows, hd * MLA_V:(hd + 1) * MLA_V]))
                     for hd, ((m_new, alpha, l_new, p), (_, _, acc)) in enumerate(zip(stage, carry)))

    carry = tuple((jnp.full((blk, 1), -jnp.inf, F32), jnp.zeros((blk, 1), F32),
                   jnp.zeros((blk, MLA_V), F32)) for _ in range(heads))
    carry = blocks(i, carry, allowed)
    carry = lax.fori_loop(0, i, lambda j, c: blocks(j, c, None), carry)
    for hd in range(heads):
        cols = slice(hd * MLA_V, (hd + 1) * MLA_V)
        _, l, acc = carry[hd]
        o_ref[:, cols] = _rms(acc / l, gn_ref[:, cols]).astype(o_ref.dtype)


def _mla_attention(q, k, v, gn, batch, blk, heads):
    t = q.shape[0]
    seq = t // batch
    nq = seq // blk
    qk_w = heads * MLA_QK_PAD
    v_w = heads * MLA_V
    return pl.pallas_call(
        functools.partial(_mla_kernel, blk=blk, heads=heads),
        grid=(batch, MLA_HEADS // heads, nq),
        in_specs=[
            pl.BlockSpec((blk, qk_w), lambda b, h, i: (b * nq + i, h)),
            pl.BlockSpec((seq, qk_w), lambda b, h, i: (b, h)),
            pl.BlockSpec((seq, v_w), lambda b, h, i: (b, h)),
            pl.BlockSpec((1, v_w), lambda b, h, i: (0, h)),
        ],
        out_specs=pl.BlockSpec((blk, v_w), lambda b, h, i: (b * nq + i, h)),
        out_shape=jax.ShapeDtypeStruct((t, MLA_HEADS * MLA_V), BF16),
        compiler_params=_cparams(("parallel", "parallel", "arbitrary")),
        name="mla_attention",
    )(q, k, v, gn)


def _out_proj_kernel(x_ref, a_ref, b_ref, c_ref, w_ref, o_ref):
    ka, kb = a_ref.shape[1], b_ref.shape[1]
    o_ref[...] = (x_ref[...] + _dot(a_ref[...], w_ref[:ka]) + _dot(b_ref[...], w_ref[ka:ka + kb])
                  + _dot(c_ref[...], w_ref[ka + kb:]))


def _out_proj(x, a, b, c, w, layer, tm, tn):
    t, d = x.shape
    act = lambda arr: pl.BlockSpec((tm, arr.shape[1]), lambda i, j: (i, 0))
    tile = pl.BlockSpec((tm, tn), lambda i, j: (i, j))
    return pl.pallas_call(
        _out_proj_kernel,
        grid=(t // tm, d // tn),
        in_specs=[tile, act(a), act(b), act(c),
                  pl.BlockSpec((None, w.shape[1], tn), lambda i, j: (layer, 0, j))],
        out_specs=tile,
        out_shape=jax.ShapeDtypeStruct((t, d), F32),
        compiler_params=_cparams(("parallel", "parallel")),
        name="out_proj",
    )(x, a, b, c, w)


def _ffn_kernel(x_ref, g_ref, wg_ref, wu_ref, wd_ref, gf_ref, o_ref, h_ref, *, final_norm):
    j = pl.program_id(1)

    @pl.when(j == 0)
    def _():
        x = x_ref[...]
        h_ref[...] = _rms(x, g_ref[...]).astype(BF16)
        o_ref[...] = x

    h = h_ref[...]
    gate = _dot(h, wg_ref[...])
    up = _dot(h, wu_ref[...])
    act = (gate * (1.0 / (1.0 + jnp.exp(-gate))) * up).astype(BF16)
    o_ref[...] += _dot(act, wd_ref[...])

    if final_norm:
        @pl.when(j == pl.num_programs(1) - 1)
        def _():
            o_ref[...] = _rms(o_ref[...], gf_ref[...])


def _ffn(x, g, wg, wu, wd, gf, layer, final_norm, tm, tf):
    t, d = x.shape
    ff = wg.shape[2]
    row = pl.BlockSpec((tm, d), lambda i, j: (i, 0))
    vec = pl.BlockSpec((1, d), lambda i, j: (0, 0))
    return pl.pallas_call(
        functools.partial(_ffn_kernel, final_norm=final_norm),
        grid=(t // tm, ff // tf),
        in_specs=[row, vec, pl.BlockSpec((None, d, tf), lambda i, j: (layer, 0, j)),
                  pl.BlockSpec((None, d, tf), lambda i, j: (layer, 0, j)),
                  pl.BlockSpec((None, tf, d), lambda i, j: (layer, j, 0)), vec],
        out_specs=row,
        out_shape=jax.ShapeDtypeStruct((t, d), F32),
        scratch_shapes=[pltpu.VMEM((tm, d), BF16)],
        compiler_params=_cparams(("parallel", "arbitrary")),
        name="ffn",
    )(x, g, wg, wu, wd, gf)


def _rotate_half_cols(w):
    half = MLA_ROPE // 2
    return jnp.concatenate([-w[..., half:], w[..., :half]], axis=-1)


def _rope_tables(positions, scale):
    inv_freq = ROPE_THETA ** (-jnp.arange(0, MLA_ROPE, 2, dtype=F32) / MLA_ROPE)
    ang = positions.astype(F32).reshape(-1, 1) * inv_freq
    cos2 = jnp.concatenate([jnp.cos(ang)] * 2, axis=-1)
    sin2 = jnp.concatenate([jnp.sin(ang)] * 2, axis=-1)
    t = ang.shape[0]
    pad = jnp.zeros((t, MLA_QK_PAD - MLA_NOPE - MLA_ROPE), F32)
    cq = jnp.concatenate([jnp.ones((t, MLA_NOPE), F32), cos2, pad], axis=-1) * scale
    sq = jnp.concatenate([jnp.zeros((t, MLA_NOPE), F32), sin2, pad], axis=-1) * scale
    ck = jnp.concatenate([cos2, jnp.zeros((t, LANES - MLA_ROPE), F32)], axis=-1)
    sk = jnp.concatenate([sin2, jnp.zeros((t, LANES - MLA_ROPE), F32)], axis=-1)
    return cq, sq, ck, sk


def _mla_weights(w_in_mla, w_uq, w_ukv):
    d = w_in_mla.shape[0]
    w_cq = w_in_mla[:, :MLA_Q_RANK]
    w_ckv = w_in_mla[:, MLA_Q_RANK:MLA_Q_RANK + MLA_KV_RANK]
    w_kr = w_in_mla[:, MLA_Q_RANK + MLA_KV_RANK:]
    zpad = jnp.zeros((d, LANES - MLA_ROPE), F32)
    w_in = jnp.concatenate([w_cq, w_ckv, w_kr, zpad, _rotate_half_cols(w_kr), zpad], axis=1)

    uq = w_uq.reshape(MLA_Q_RANK, MLA_HEADS, MLA_NOPE + MLA_ROPE)
    nope, rope = uq[..., :MLA_NOPE], uq[..., MLA_NOPE:]
    zr = jnp.zeros((MLA_Q_RANK, MLA_HEADS, MLA_QK_PAD - MLA_NOPE - MLA_ROPE), F32)
    wa = jnp.concatenate([nope, rope, zr], axis=-1).reshape(MLA_Q_RANK, MLA_HEADS * MLA_QK_PAD)
    wb = jnp.concatenate([jnp.zeros_like(nope), _rotate_half_cols(rope), zr], axis=-1)
    wb = wb.reshape(MLA_Q_RANK, MLA_HEADS * MLA_QK_PAD)

    ukv = w_ukv.reshape(MLA_KV_RANK, MLA_HEADS, MLA_NOPE + MLA_V)
    wk = ukv[..., :MLA_NOPE].reshape(MLA_KV_RANK, MLA_HEADS * MLA_NOPE)
    wv = ukv[..., MLA_NOPE:].reshape(MLA_KV_RANK, MLA_HEADS * MLA_V)
    return tuple(w.astype(BF16) for w in (w_in, wa, wb, wk, wv))


def kernel(x, positions, attn_norm_g, w_in, hg_lower_bounds, hg_norm_g, sb_norm_g, mla_q_norm_g,
           mla_w_uq, mla_kv_norm_g, mla_w_ukv, mla_out_norm_g, w_o, ffn_norm_g, w_gate, w_up,
           w_down, final_norm_g):
    batch, seq, d_model = x.shape
    depth = w_in.shape[0]
    t = batch * seq
    hg_w = HG_HEADS * HG_D
    sb_w = SB_HEADS * SB_D
    mla_w = MLA_HEADS * MLA_V
    hg_cols = 4 * hg_w
    sb_cols = 3 * sb_w

    sm = jax.nn.softmax(hg_lower_bounds.astype(F32), axis=0)
    lower_bounds = jnp.cumsum(sm, axis=0) - sm[0:1]
    tables = _rope_tables(positions, (MLA_NOPE + MLA_ROPE) ** -0.5 * LOG2E)

    wo, wg, wu, wd = (w.astype(BF16) for w in (w_o, w_gate, w_up, w_down))

    xt = x.reshape(t, d_model)
    for layer in range(depth):
        g_attn = attn_norm_g[layer].reshape(1, d_model)
        w_l = w_in[layer]
        w_hg = w_l[:, :hg_cols].astype(BF16)
        w_sb = w_l[:, hg_cols:hg_cols + sb_cols].astype(BF16)
        mla_ws = _mla_weights(w_l[:, hg_cols + sb_cols:], mla_w_uq[layer], mla_w_ukv[layer])

        hg_proj, h = _norm_proj(xt, g_attn, w_hg, tm=min(t, 512), tn=hg_cols)
        sb_proj = _proj(h, w_sb, -(SB_D ** -0.5), tm=min(t, 1024), tn=sb_w)
        mq, mk, mv = _mla_proj(h, mla_ws[0], mla_q_norm_g[layer].reshape(1, -1), mla_ws[1],
                               mla_ws[2], mla_kv_norm_g[layer].reshape(1, -1), mla_ws[3], mla_ws[4],
                               *tables, tm=min(t, 512))

        hg_out = _hgrn(hg_proj, lower_bounds[layer].reshape(1, hg_w), hg_norm_g[layer].reshape(1, hg_w),
                       batch, tb=min(seq, 512))
        sb_out = _sb_attention(sb_proj, sb_norm_g[layer].reshape(1, sb_w), batch, blk=min(seq, 256),
                               heads=ATTN_HEADS_PER_STEP)
        mla_out = _mla_attention(mq, mk, mv, mla_out_norm_g[layer].reshape(1, mla_w), batch,
                                 blk=min(seq, 256), heads=ATTN_HEADS_PER_STEP)

        xt = _out_proj(xt, hg_out, sb_out, mla_out, wo, layer, tm=min(t, 512), tn=d_model)
        xt = _ffn(xt, ffn_norm_g[layer].reshape(1, d_model), wg, wu, wd, final_norm_g.reshape(1, d_model),
                  layer, final_norm=(layer == depth - 1), tm=min(t, 1024), tf=256)
    return xt.reshape(batch, seq, d_model)
```

```python
import functools

import jax
import jax.numpy as jnp
from jax import lax
from jax.experimental import pallas as pl
from jax.experimental.pallas import tpu as pltpu

F32 = jnp.float32
BF16 = jnp.bfloat16

NORM_EPS = 1e-6
LOG2E = 1.4426950408889634
ROPE_THETA = 10000.0
CHUNK = 64

HG_HEADS = 4
HG_D = 128
SB_HEADS = 6
SB_D = 128
MLA_HEADS = 6
MLA_NOPE = 128
MLA_ROPE = 64
MLA_V = 128
MLA_Q_RANK = 512
MLA_KV_RANK = 256
MLA_QK_PAD = 256

ATTN_HEADS_PER_STEP = 6
HG_CHUNK = 64
HG_SUB = 16
SUBLANES = 8
LANES = 128
VMEM_LIMIT = 56 * 1024 * 1024


def _cparams(semantics):
    return pltpu.CompilerParams(dimension_semantics=semantics, vmem_limit_bytes=VMEM_LIMIT)


def _rms(x, g):
    return x * lax.rsqrt(jnp.mean(x * x, axis=-1, keepdims=True) + NORM_EPS) * g


def _dot(a, b):
    return jnp.dot(a, b, preferred_element_type=F32)


def _row_bcast(col):
    return jnp.broadcast_to(col, (col.shape[0], LANES))


def _tile_lanes(x, width):
    return jnp.concatenate([x] * (width // LANES), axis=1)


def _dot_nt(a, b):
    return lax.dot_general(a, b, (((1,), (1,)), ((), ())), preferred_element_type=F32)


def _dot_tn(a, b):
    return lax.dot_general(a, b, (((0,), (0,)), ((), ())), preferred_element_type=F32)


def _norm_proj_kernel(x_ref, g_ref, w_ref, o_ref, h_ref):
    @pl.when(pl.program_id(1) == 0)
    def _():
        h_ref[...] = _rms(x_ref[...], g_ref[...]).astype(BF16)

    o_ref[...] = _dot_nt(h_ref[...], w_ref[...])


def _norm_proj(x, g, w_t, tm, tn):
    t, d = x.shape
    n = w_t.shape[0]
    return pl.pallas_call(
        _norm_proj_kernel,
        grid=(t // tm, n // tn),
        in_specs=[
            pl.BlockSpec((tm, d), lambda i, j: (i, 0)),
            pl.BlockSpec((1, d), lambda i, j: (0, 0)),
            pl.BlockSpec((tn, d), lambda i, j: (j, 0)),
        ],
        out_specs=[pl.BlockSpec((tm, tn), lambda i, j: (i, j)), pl.BlockSpec((tm, d), lambda i, j: (i, 0))],
        out_shape=[jax.ShapeDtypeStruct((t, n), F32), jax.ShapeDtypeStruct((t, d), BF16)],
        compiler_params=_cparams(("parallel", "arbitrary")),
        name="norm_proj",
    )(x, g, w_t)


def _proj_kernel(h_ref, w_ref, o_ref, *, first_tile_scale):
    factor = jnp.where(pl.program_id(1) == 0, first_tile_scale, 1.0)
    o_ref[...] = (_dot_nt(h_ref[...], w_ref[...]) * factor).astype(o_ref.dtype)


def _proj(h, w_t, first_tile_scale, tm, tn):
    t, d = h.shape
    n = w_t.shape[0]
    return pl.pallas_call(
        functools.partial(_proj_kernel, first_tile_scale=first_tile_scale),
        grid=(t // tm, n // tn),
        in_specs=[pl.BlockSpec((tm, d), lambda i, j: (i, 0)), pl.BlockSpec((tn, d), lambda i, j: (j, 0))],
        out_specs=pl.BlockSpec((tm, tn), lambda i, j: (i, j)),
        out_shape=jax.ShapeDtypeStruct((t, n), BF16),
        compiler_params=_cparams(("parallel", "parallel")),
        name="proj",
    )(h, w_t)


def _mla_proj_kernel(h_ref, win_ref, gq_ref, wa_ref, wb_ref, gkv_ref, wk_ref, wv_ref,
                     cq_ref, sq_ref, ck_ref, sk_ref, q_ref, k_ref, v_ref):
    p = _dot_nt(h_ref[...], win_ref[...])
    c_q = p[:, :MLA_Q_RANK]
    c_kv = p[:, MLA_Q_RANK:MLA_Q_RANK + MLA_KV_RANK]
    k_rope = p[:, MLA_Q_RANK + MLA_KV_RANK:MLA_Q_RANK + MLA_KV_RANK + LANES]
    k_rope_rot = p[:, MLA_Q_RANK + MLA_KV_RANK + LANES:]

    cqn = _rms(c_q, gq_ref[...]).astype(BF16)
    qa = _dot(cqn, wa_ref[...])
    qb = _dot(cqn, wb_ref[...])
    cq = cq_ref[...]
    sq = sq_ref[...]
    for hd in range(MLA_HEADS):
        sl = slice(hd * MLA_QK_PAD, (hd + 1) * MLA_QK_PAD)
        q_ref[:, sl] = (qa[:, sl] * cq + qb[:, sl] * sq).astype(BF16)

    ckvn = _rms(c_kv, gkv_ref[...]).astype(BF16)
    kn = _dot(ckvn, wk_ref[...])
    v_ref[...] = _dot(ckvn, wv_ref[...]).astype(BF16)
    kr = (k_rope * ck_ref[...] + k_rope_rot * sk_ref[...]).astype(BF16)
    for hd in range(MLA_HEADS):
        k_ref[:, hd * MLA_QK_PAD:hd * MLA_QK_PAD + MLA_NOPE] = (
            kn[:, hd * MLA_NOPE:(hd + 1) * MLA_NOPE].astype(BF16))
        k_ref[:, hd * MLA_QK_PAD + MLA_NOPE:(hd + 1) * MLA_QK_PAD] = kr


def _mla_proj(h, w_in, gq, wa, wb, gkv, wk, wv, cq, sq, ck, sk, tm):
    t, d = h.shape
    const = lambda i: (0, 0)
    row = lambda i: (i, 0)
    full = lambda a: pl.BlockSpec(a.shape, const)
    qk_w = MLA_HEADS * MLA_QK_PAD
    v_w = MLA_HEADS * MLA_V
    return pl.pallas_call(
        _mla_proj_kernel,
        grid=(t // tm,),
        in_specs=[
            pl.BlockSpec((tm, d), row), full(w_in), full(gq), full(wa), full(wb),
            full(gkv), full(wk), full(wv),
            pl.BlockSpec((tm, MLA_QK_PAD), row), pl.BlockSpec((tm, MLA_QK_PAD), row),
            pl.BlockSpec((tm, LANES), row), pl.BlockSpec((tm, LANES), row),
        ],
        out_specs=[
            pl.BlockSpec((tm, qk_w), row), pl.BlockSpec((tm, qk_w), row), pl.BlockSpec((tm, v_w), row),
        ],
        out_shape=[
            jax.ShapeDtypeStruct((t, qk_w), BF16), jax.ShapeDtypeStruct((t, qk_w), BF16),
            jax.ShapeDtypeStruct((t, v_w), BF16),
        ],
        compiler_params=_cparams(("parallel",)),
        name="mla_proj",
    )(h, w_in, gq, wa, wb, gkv, wk, wv, cq, sq, ck, sk)


def _hgrn_diag_scores(q, g2, h_ref):
    lane = lax.broadcasted_iota(jnp.int32, (SUBLANES, LANES), 1)
    row1 = lax.broadcasted_iota(jnp.int32, (SUBLANES, 1), 0)
    out = []
    for tv in range(HG_CHUNK // SUBLANES):
        t0 = tv * SUBLANES
        s0 = (t0 // HG_SUB) * HG_SUB
        qv = q[t0:t0 + SUBLANES]
        gv = g2[t0:t0 + SUBLANES]
        acc = jnp.zeros((SUBLANES, LANES), F32)
        for s in range(s0, t0 + SUBLANES):
            m = qv * jnp.exp2(gv - h_ref[s:s + 1, :])
            r = jnp.sum(m, axis=-1, keepdims=True)
            if s >= t0:
                r = jnp.where(row1 + t0 >= s, r, 0.0)
            acc = jnp.where(lane == s, r, acc)
        out.append(acc)
    return jnp.concatenate(out, axis=0)


def _hgrn_kernel(q_ref, f_ref, v_ref, gate_ref, lb_ref, gn_ref, o_ref, st_ref, g_sc, h_sc):
    c = HG_CHUNK
    n_sub = c // HG_SUB
    n_chunks = q_ref.shape[0] // c
    heads = range(HG_HEADS)
    cols = [slice(hd * HG_D, (hd + 1) * HG_D) for hd in heads]

    @pl.when(pl.program_id(1) == 0)
    def _():
        st_ref[...] = jnp.zeros_like(st_ref)

    tri = (lax.broadcasted_iota(jnp.int32, (c, c), 0)
           >= lax.broadcasted_iota(jnp.int32, (c, c), 1)).astype(BF16)
    zeros_sub = jnp.zeros((HG_SUB, LANES), F32)

    def chunk(ci, _):
        rows = pl.ds(pl.multiple_of(ci * c, c), c)

        ks, pieces = [], []
        for cs in cols:
            fp = f_ref[rows, cs]
            lb = lb_ref[:, cs]
            e = jnp.exp(-jnp.abs(fp))
            r = 1.0 / (1.0 + e)
            sig = jnp.where(fp >= 0, r, e * r)
            nsig = jnp.where(fp >= 0, e * r, r)
            log_f = jnp.log(lb + (1.0 - lb) * sig)
            ks.append((1.0 - lb) * nsig)
            p1 = log_f.astype(BF16)
            r1 = log_f - p1.astype(F32)
            p2 = r1.astype(BF16)
            p3 = (r1 - p2.astype(F32)).astype(BF16)
            pieces.append(jnp.concatenate([p1, p2, p3], axis=1))
        gs = []
        for p in pieces:
            r = _dot(tri, p)
            gs.append(r[:, :HG_D] + r[:, HG_D:2 * HG_D] + r[:, 2 * HG_D:])

        qs, vs, g2s, q_in, k_out, q_hat, k_hat = [], [], [], [], [], [], []
        for hd, cs in enumerate(cols):
            g, k = gs[hd], ks[hd]
            g2 = g * LOG2E
            g_sc[hd] = g
            h_sc[hd] = g2 - jnp.log(k) * LOG2E
            g2s.append(g2)
            q = q_ref[rows, cs]
            qs.append(q)
            vs.append(v_ref[rows, cs].astype(BF16))
            g_last = g_sc[hd, c - 1:c, :]
            q_in.append((q * jnp.exp(g)).astype(BF16))
            k_out.append((k * jnp.exp(g_last - g)).astype(BF16))
            qh, kh = [], []
            for sb in range(1, n_sub):
                n = sb * HG_SUB
                g_prev = g_sc[hd, n - 1:n, :]
                qh.append((q[n:n + HG_SUB] * jnp.exp(g[n:n + HG_SUB] - g_prev)).astype(BF16))
                kh.append(jnp.concatenate(
                    [k[:n] * jnp.exp(g_prev - g[:n])] + [zeros_sub] * (n_sub - sb), axis=0).astype(BF16))
            q_hat.append(qh)
            k_hat.append(kh)

        off, inter = [], []
        for hd in heads:
            off.append(jnp.concatenate(
                [zeros_sub[:, :c]] + [_dot_nt(qh, kh) for qh, kh in zip(q_hat[hd], k_hat[hd])], axis=0))
            st = st_ref[hd]
            inter.append(_dot_nt(q_in[hd], st.astype(BF16)))
            g_last = g_sc[hd, c - 1:c, :]
            st_ref[hd] = st * jnp.exp(g_last) + _dot_tn(vs[hd], k_out[hd])

        diag = [_hgrn_diag_scores(qs[hd], g2s[hd], h_sc.at[hd]) for hd in heads]
        outs = [_dot((diag[hd][:, :c] + off[hd]).astype(BF16), vs[hd]) + inter[hd] for hd in heads]

        for hd, cs in enumerate(cols):
            y = _rms(outs[hd], gn_ref[:, cs])
            gate = gate_ref[rows, cs]
            o_ref[rows, cs] = (y * gate * (1.0 / (1.0 + jnp.exp(-gate)))).astype(o_ref.dtype)
        return 0

    lax.fori_loop(0, n_chunks, chunk, 0)


def _hgrn(proj, lb, gn, batch, tb):
    t = proj.shape[0]
    nt = t // batch // tb
    width = HG_HEADS * HG_D
    col = lambda off: pl.BlockSpec((tb, width), lambda b, i: (b * nt + i, off))
    vec = pl.BlockSpec((1, width), lambda b, i: (0, 0))
    per_head = lambda rows: pltpu.VMEM((HG_HEADS, rows, HG_D), F32)
    return pl.pallas_call(
        _hgrn_kernel,
        grid=(batch, nt),
        in_specs=[col(0), col(1), col(2), col(3), vec, vec],
        out_specs=pl.BlockSpec((tb, width), lambda b, i: (b * nt + i, 0)),
        out_shape=jax.ShapeDtypeStruct((t, width), BF16),
        scratch_shapes=[per_head(HG_D), per_head(HG_CHUNK), per_head(HG_CHUNK)],
        compiler_params=_cparams(("parallel", "arbitrary")),
        name="hgrn2",
    )(proj, proj, proj, proj, lb, gn)


def _sb_kernel(q_ref, k_ref, v_ref, u_ref, gn_ref, o_ref, later_sc, acc_sc, *, blk, heads):
    i = pl.program_id(2)
    u = u_ref[...]
    row = lax.broadcasted_iota(jnp.int32, (blk, blk), 0)
    col = lax.broadcasted_iota(jnp.int32, (blk, blk), 1)
    strict = col < row
    cols = [slice(hd * SB_D, (hd + 1) * SB_D) for hd in range(heads)]

    def blocks(kb, first):
        rows = pl.ds(pl.multiple_of(kb * blk, blk), blk)
        nzs = [_dot_nt(q_ref[:, c], k_ref[rows, c]) for c in cols]
        stage = []
        for nz in nzs:
            e = jnp.exp2(jnp.abs(nz) * -LOG2E)
            log_keep = jnp.minimum(nz, 0.0) - jnp.log(1.0 + e)
            if first:
                log_keep = jnp.where(strict, log_keep, 0.0)
            stage.append((log_keep.astype(BF16), _row_bcast(jnp.sum(log_keep, axis=-1, keepdims=True))))
        withins = [_dot(lk, u) for lk, _ in stage]
        weights = []
        for hd, (nz, within) in enumerate(zip(nzs, withins)):
            if first:
                a = jnp.where(strict, jnp.exp(within - nz), 0.0)
            else:
                a = jnp.exp(within + _tile_lanes(later_sc[hd], blk) - nz)
            weights.append(a.astype(BF16))
        for hd, (a, c) in enumerate(zip(weights, cols)):
            out = _dot(a, v_ref[rows, c])
            if first:
                later_sc[hd] = stage[hd][1]
                acc_sc[hd] = out
            else:
                later_sc[hd] += stage[hd][1]
                acc_sc[hd] += out

    def step(kb):
        blocks(kb, False)
        return 0

    blocks(i, True)
    odd = i % 2
    top = i - 1 - odd
    lax.fori_loop(0, odd, lambda j, _: step(i - 1), 0)
    lax.fori_loop(0, i // 2, lambda j, _: step(top - 2 * j) + step(top - 2 * j - 1), 0)
    for hd, c in enumerate(cols):
        o_ref[:, c] = _rms(acc_sc[hd], gn_ref[:, c]).astype(o_ref.dtype)


def _sb_attention(qkv, gn, batch, blk, heads):
    t = qkv.shape[0]
    seq = t // batch
    nq = seq // blk
    groups = SB_HEADS // heads
    width = heads * SB_D
    u = (lax.broadcasted_iota(jnp.int32, (blk, blk), 0)
         >= lax.broadcasted_iota(jnp.int32, (blk, blk), 1)).astype(BF16)
    q_spec = pl.BlockSpec((blk, width), lambda b, h, i: (b * nq + i, h))
    k_spec = pl.BlockSpec((seq, width), lambda b, h, i: (b, groups + h))
    v_spec = pl.BlockSpec((seq, width), lambda b, h, i: (b, 2 * groups + h))
    return pl.pallas_call(
        functools.partial(_sb_kernel, blk=blk, heads=heads),
        grid=(batch, groups, nq),
        in_specs=[q_spec, k_spec, v_spec, pl.BlockSpec((blk, blk), lambda b, h, i: (0, 0)),
                  pl.BlockSpec((1, width), lambda b, h, i: (0, h))],
        out_specs=q_spec,
        out_shape=jax.ShapeDtypeStruct((t, SB_HEADS * SB_D), BF16),
        scratch_shapes=[pltpu.VMEM((heads, blk, LANES), F32), pltpu.VMEM((heads, blk, SB_D), F32)],
        compiler_params=_cparams(("parallel", "parallel", "arbitrary")),
        name="sb_attention",
    )(qkv, qkv, qkv, u, gn)


def _mla_kernel(q_ref, k_ref, v_ref, gn_ref, o_ref, m_sc, l_sc, acc_sc, *, blk, heads):
    i = pl.program_id(2)
    row = lax.broadcasted_iota(jnp.int32, (blk, blk), 0)
    col = lax.broadcasted_iota(jnp.int32, (blk, blk), 1)
    allowed = (col // CHUNK) <= (row // CHUNK)

    def blocks(kb, first):
        rows = pl.ds(pl.multiple_of(kb * blk, blk), blk)
        scores = [_dot_nt(q_ref[:, hd * MLA_QK_PAD:(hd + 1) * MLA_QK_PAD],
                          k_ref[rows, hd * MLA_QK_PAD:(hd + 1) * MLA_QK_PAD]) for hd in range(heads)]
        stage = []
        for hd, s in enumerate(scores):
            if first:
                s = jnp.where(allowed, s, -jnp.inf)
                m_new = _row_bcast(jnp.max(s, axis=-1, keepdims=True))
                alpha = None
            else:
                m_old = m_sc[hd]
                m_new = jnp.maximum(m_old, _row_bcast(jnp.max(s, axis=-1, keepdims=True)))
                alpha = jnp.exp2(m_old - m_new)
            p = jnp.exp2(s - _tile_lanes(m_new, blk))
            m_sc[hd] = m_new
            stage.append((alpha, _row_bcast(jnp.sum(p, axis=-1, keepdims=True)), p.astype(BF16)))
        for hd, (alpha, p_sum, p) in enumerate(stage):
            out = _dot(p, v_ref[rows, hd * MLA_V:(hd + 1) * MLA_V])
            if first:
                l_sc[hd] = p_sum
                acc_sc[hd] = out
            else:
                l_sc[hd] = alpha * l_sc[hd] + p_sum
                acc_sc[hd] = alpha * acc_sc[hd] + out

    def step(kb):
        blocks(kb, False)
        return 0

    blocks(i, True)
    odd = i % 2
    lax.fori_loop(0, odd, lambda j, _: step(0), 0)
    lax.fori_loop(0, i // 2, lambda j, _: step(odd + 2 * j) + step(odd + 2 * j + 1), 0)
    for hd in range(heads):
        cols = slice(hd * MLA_V, (hd + 1) * MLA_V)
        o_ref[:, cols] = _rms(acc_sc[hd] / l_sc[hd], gn_ref[:, cols]).astype(o_ref.dtype)


def _mla_attention(q, k, v, gn, batch, blk, heads):
    t = q.shape[0]
    seq = t // batch
    nq = seq // blk
    qk_w = heads * MLA_QK_PAD
    v_w = heads * MLA_V
    return pl.pallas_call(
        functools.partial(_mla_kernel, blk=blk, heads=heads),
        grid=(batch, MLA_HEADS // heads, nq),
        in_specs=[
            pl.BlockSpec((blk, qk_w), lambda b, h, i: (b * nq + i, h)),
            pl.BlockSpec((seq, qk_w), lambda b, h, i: (b, h)),
            pl.BlockSpec((seq, v_w), lambda b, h, i: (b, h)),
            pl.BlockSpec((1, v_w), lambda b, h, i: (0, h)),
        ],
        out_specs=pl.BlockSpec((blk, v_w), lambda b, h, i: (b * nq + i, h)),
        out_shape=jax.ShapeDtypeStruct((t, MLA_HEADS * MLA_V), BF16),
        scratch_shapes=[pltpu.VMEM((heads, blk, LANES), F32), pltpu.VMEM((heads, blk, LANES), F32),
                        pltpu.VMEM((heads, blk, MLA_V), F32)],
        compiler_params=_cparams(("parallel", "parallel", "arbitrary")),
        name="mla_attention",
    )(q, k, v, gn)


def _out_proj_kernel(x_ref, a_ref, b_ref, c_ref, w_ref, o_ref):
    ka, kb = a_ref.shape[1], b_ref.shape[1]
    o_ref[...] = (x_ref[...] + _dot(a_ref[...], w_ref[:ka]) + _dot(b_ref[...], w_ref[ka:ka + kb])
                  + _dot(c_ref[...], w_ref[ka + kb:]))


def _out_proj(x, a, b, c, w, layer, tm, tn):
    t, d = x.shape
    act = lambda arr: pl.BlockSpec((tm, arr.shape[1]), lambda i, j: (i, 0))
    tile = pl.BlockSpec((tm, tn), lambda i, j: (i, j))
    return pl.pallas_call(
        _out_proj_kernel,
        grid=(t // tm, d // tn),
        in_specs=[tile, act(a), act(b), act(c),
                  pl.BlockSpec((None, w.shape[1], tn), lambda i, j: (layer, 0, j))],
        out_specs=tile,
        out_shape=jax.ShapeDtypeStruct((t, d), F32),
        compiler_params=_cparams(("parallel", "parallel")),
        name="out_proj",
    )(x, a, b, c, w)


def _ffn_kernel(x_ref, g_ref, wg_ref, wu_ref, wd_ref, gf_ref, o_ref, h_ref, *, final_norm):
    j = pl.program_id(1)

    @pl.when(j == 0)
    def _():
        x = x_ref[...]
        h_ref[...] = _rms(x, g_ref[...]).astype(BF16)
        o_ref[...] = x

    h = h_ref[...]
    gate = _dot(h, wg_ref[...])
    up = _dot(h, wu_ref[...])
    act = (gate * (1.0 / (1.0 + jnp.exp(-gate))) * up).astype(BF16)
    o_ref[...] += _dot(act, wd_ref[...])

    if final_norm:
        @pl.when(j == pl.num_programs(1) - 1)
        def _():
            o_ref[...] = _rms(o_ref[...], gf_ref[...])


def _ffn(x, g, wg, wu, wd, gf, layer, final_norm, tm, tf):
    t, d = x.shape
    ff = wg.shape[2]
    row = pl.BlockSpec((tm, d), lambda i, j: (i, 0))
    vec = pl.BlockSpec((1, d), lambda i, j: (0, 0))
    return pl.pallas_call(
        functools.partial(_ffn_kernel, final_norm=final_norm),
        grid=(t // tm, ff // tf),
        in_specs=[row, vec, pl.BlockSpec((None, d, tf), lambda i, j: (layer, 0, j)),
                  pl.BlockSpec((None, d, tf), lambda i, j: (layer, 0, j)),
                  pl.BlockSpec((None, tf, d), lambda i, j: (layer, j, 0)), vec],
        out_specs=row,
        out_shape=jax.ShapeDtypeStruct((t, d), F32),
        scratch_shapes=[pltpu.VMEM((tm, d), BF16)],
        compiler_params=_cparams(("parallel", "arbitrary")),
        name="ffn",
    )(x, g, wg, wu, wd, gf)


def _rotate_half_cols(w):
    half = MLA_ROPE // 2
    return jnp.concatenate([-w[..., half:], w[..., :half]], axis=-1)


def _rope_tables(positions, scale):
    inv_freq = ROPE_THETA ** (-jnp.arange(0, MLA_ROPE, 2, dtype=F32) / MLA_ROPE)
    ang = positions.astype(F32).reshape(-1, 1) * inv_freq
    cos2 = jnp.concatenate([jnp.cos(ang)] * 2, axis=-1)
    sin2 = jnp.concatenate([jnp.sin(ang)] * 2, axis=-1)
    t = ang.shape[0]
    pad = jnp.zeros((t, MLA_QK_PAD - MLA_NOPE - MLA_ROPE), F32)
    cq = jnp.concatenate([jnp.ones((t, MLA_NOPE), F32), cos2, pad], axis=-1) * scale
    sq = jnp.concatenate([jnp.zeros((t, MLA_NOPE), F32), sin2, pad], axis=-1) * scale
    ck = jnp.concatenate([cos2, jnp.zeros((t, LANES - MLA_ROPE), F32)], axis=-1)
    sk = jnp.concatenate([sin2, jnp.zeros((t, LANES - MLA_ROPE), F32)], axis=-1)
    return cq, sq, ck, sk


def _mla_weights(w_in_mla_t, w_uq, w_ukv):
    d = w_in_mla_t.shape[1]
    half = MLA_ROPE // 2
    w_kr = w_in_mla_t[MLA_Q_RANK + MLA_KV_RANK:]
    w_kr_rot = jnp.concatenate([-w_kr[half:], w_kr[:half]], axis=0)
    zpad = jnp.zeros((LANES - MLA_ROPE, d), F32)
    w_in = jnp.concatenate([w_in_mla_t[:MLA_Q_RANK + MLA_KV_RANK], w_kr, zpad, w_kr_rot, zpad], axis=0)

    uq = w_uq.reshape(MLA_Q_RANK, MLA_HEADS, MLA_NOPE + MLA_ROPE)
    nope, rope = uq[..., :MLA_NOPE], uq[..., MLA_NOPE:]
    zr = jnp.zeros((MLA_Q_RANK, MLA_HEADS, MLA_QK_PAD - MLA_NOPE - MLA_ROPE), F32)
    wa = jnp.concatenate([nope, rope, zr], axis=-1).reshape(MLA_Q_RANK, MLA_HEADS * MLA_QK_PAD)
    wb = jnp.concatenate([jnp.zeros_like(nope), _rotate_half_cols(rope), zr], axis=-1)
    wb = wb.reshape(MLA_Q_RANK, MLA_HEADS * MLA_QK_PAD)

    ukv = w_ukv.reshape(MLA_KV_RANK, MLA_HEADS, MLA_NOPE + MLA_V)
    wk = ukv[..., :MLA_NOPE].reshape(MLA_KV_RANK, MLA_HEADS * MLA_NOPE)
    wv = ukv[..., MLA_NOPE:].reshape(MLA_KV_RANK, MLA_HEADS * MLA_V)
    return tuple(w.astype(BF16) for w in (w_in, wa, wb, wk, wv))


def kernel(x, positions, attn_norm_g, w_in, hg_lower_bounds, hg_norm_g, sb_norm_g, mla_q_norm_g,
           mla_w_uq, mla_kv_norm_g, mla_w_ukv, mla_out_norm_g, w_o, ffn_norm_g, w_gate, w_up,
           w_down, final_norm_g):
    batch, seq, d_model = x.shape
    depth = w_in.shape[0]
    t = batch * seq
    hg_w = HG_HEADS * HG_D
    sb_w = SB_HEADS * SB_D
    mla_w = MLA_HEADS * MLA_V
    hg_cols = 4 * hg_w
    sb_cols = 3 * sb_w

    sm = jax.nn.softmax(hg_lower_bounds.astype(F32), axis=0)
    lower_bounds = jnp.cumsum(sm, axis=0) - sm[0:1]
    tables = _rope_tables(positions, (MLA_NOPE + MLA_ROPE) ** -0.5 * LOG2E)

    wo, wg, wu, wd = (w.astype(BF16) for w in (w_o, w_gate, w_up, w_down))
    w_in_t = jnp.swapaxes(w_in, 1, 2)

    xt = x.reshape(t, d_model)
    for layer in range(depth):
        g_attn = attn_norm_g[layer].reshape(1, d_model)
        w_l = w_in_t[layer]
        w_hg = w_l[:hg_cols].astype(BF16)
        w_sb = w_l[hg_cols:hg_cols + sb_cols].astype(BF16)
        mla_ws = _mla_weights(w_l[hg_cols + sb_cols:], mla_w_uq[layer], mla_w_ukv[layer])

        hg_proj, h = _norm_proj(xt, g_attn, w_hg, tm=min(t, 512), tn=hg_cols)
        sb_proj = _proj(h, w_sb, -(SB_D ** -0.5), tm=min(t, 1024), tn=sb_w)
        mq, mk, mv = _mla_proj(h, mla_ws[0], mla_q_norm_g[layer].reshape(1, -1), mla_ws[1],
                               mla_ws[2], mla_kv_norm_g[layer].reshape(1, -1), mla_ws[3], mla_ws[4],
                               *tables, tm=min(t, 512))

        hg_out = _hgrn(hg_proj, lower_bounds[layer].reshape(1, hg_w), hg_norm_g[layer].reshape(1, hg_w),
                       batch, tb=min(seq, 512))
        sb_out = _sb_attention(sb_proj, sb_norm_g[layer].reshape(1, sb_w), batch, blk=min(seq, 256),
                               heads=ATTN_HEADS_PER_STEP)
        mla_out = _mla_attention(mq, mk, mv, mla_out_norm_g[layer].reshape(1, mla_w), batch,
                                 blk=min(seq, 256), heads=ATTN_HEADS_PER_STEP)

        xt = _out_proj(xt, hg_out, sb_out, mla_out, wo, layer, tm=min(t, 512), tn=d_model)
        xt = _ffn(xt, ffn_norm_g[layer].reshape(1, d_model), wg, wu, wd, final_norm_g.reshape(1, d_model),
                  layer, final_norm=(layer == depth - 1), tm=min(t, 1024), tf=512)
    return xt.reshape(batch, seq, d_model)
```

```python
import functools

import jax
import jax.numpy as jnp
from jax import lax
from jax.experimental import pallas as pl
from jax.experimental.pallas import tpu as pltpu

F32 = jnp.float32
BF16 = jnp.bfloat16

NORM_EPS = 1e-6
LOG2E = 1.4426950408889634
ROPE_THETA = 10000.0
CHUNK = 64

HG_HEADS = 4
HG_D = 128
SB_HEADS = 6
SB_D = 128
MLA_HEADS = 6
MLA_NOPE = 128
MLA_ROPE = 64
MLA_V = 128
MLA_Q_RANK = 512
MLA_KV_RANK = 256
MLA_QK_PAD = 256

ATTN_HEADS_PER_STEP = 6
HG_CHUNK = 64
HG_SUB = 16
HG_UNROLL = 2
SUBLANES = 8
LANES = 128
VMEM_LIMIT = 56 * 1024 * 1024


def _cparams(semantics):
    return pltpu.CompilerParams(dimension_semantics=semantics, vmem_limit_bytes=VMEM_LIMIT)


def _rms(x, g):
    return x * lax.rsqrt(jnp.mean(x * x, axis=-1, keepdims=True) + NORM_EPS) * g


def _dot(a, b):
    return jnp.dot(a, b, preferred_element_type=F32)


def _row_bcast(col):
    return jnp.broadcast_to(col, (col.shape[0], LANES))


def _tile_lanes(x, width):
    return jnp.concatenate([x] * (width // LANES), axis=1)


def _dot_nt(a, b):
    return lax.dot_general(a, b, (((1,), (1,)), ((), ())), preferred_element_type=F32)


def _dot_tn(a, b):
    return lax.dot_general(a, b, (((0,), (0,)), ((), ())), preferred_element_type=F32)


def _norm_proj_kernel(x_ref, g_ref, w_ref, o_ref, h_ref):
    @pl.when(pl.program_id(1) == 0)
    def _():
        h_ref[...] = _rms(x_ref[...], g_ref[...]).astype(BF16)

    o_ref[...] = _dot_nt(h_ref[...], w_ref[...])


def _norm_proj(x, g, w_t, tm, tn):
    t, d = x.shape
    n = w_t.shape[0]
    return pl.pallas_call(
        _norm_proj_kernel,
        grid=(t // tm, n // tn),
        in_specs=[
            pl.BlockSpec((tm, d), lambda i, j: (i, 0)),
            pl.BlockSpec((1, d), lambda i, j: (0, 0)),
            pl.BlockSpec((tn, d), lambda i, j: (j, 0)),
        ],
        out_specs=[pl.BlockSpec((tm, tn), lambda i, j: (i, j)), pl.BlockSpec((tm, d), lambda i, j: (i, 0))],
        out_shape=[jax.ShapeDtypeStruct((t, n), F32), jax.ShapeDtypeStruct((t, d), BF16)],
        compiler_params=_cparams(("parallel", "arbitrary")),
        name="norm_proj",
    )(x, g, w_t)


def _proj_kernel(h_ref, w_ref, o_ref, *, first_tile_scale):
    factor = jnp.where(pl.program_id(1) == 0, first_tile_scale, 1.0)
    o_ref[...] = (_dot_nt(h_ref[...], w_ref[...]) * factor).astype(o_ref.dtype)


def _proj(h, w_t, first_tile_scale, tm, tn):
    t, d = h.shape
    n = w_t.shape[0]
    return pl.pallas_call(
        functools.partial(_proj_kernel, first_tile_scale=first_tile_scale),
        grid=(t // tm, n // tn),
        in_specs=[pl.BlockSpec((tm, d), lambda i, j: (i, 0)), pl.BlockSpec((tn, d), lambda i, j: (j, 0))],
        out_specs=pl.BlockSpec((tm, tn), lambda i, j: (i, j)),
        out_shape=jax.ShapeDtypeStruct((t, n), BF16),
        compiler_params=_cparams(("parallel", "parallel")),
        name="proj",
    )(h, w_t)


def _mla_proj_kernel(h_ref, win_ref, gq_ref, wa_ref, wb_ref, gkv_ref, wk_ref, wv_ref,
                     cos_ref, sin_ref, q_ref, k_ref, v_ref, *, q_scale):
    p = _dot_nt(h_ref[...], win_ref[...])
    c_q = p[:, :MLA_Q_RANK]
    c_kv = p[:, MLA_Q_RANK:MLA_Q_RANK + MLA_KV_RANK]
    k_rope = p[:, MLA_Q_RANK + MLA_KV_RANK:MLA_Q_RANK + MLA_KV_RANK + LANES]
    k_rope_rot = p[:, MLA_Q_RANK + MLA_KV_RANK + LANES:]

    cqn = _rms(c_q, gq_ref[...]).astype(BF16)
    qa = _dot(cqn, wa_ref[...])
    qb = _dot(cqn, wb_ref[...])
    cos = cos_ref[...]
    sin = sin_ref[...]
    q_cos = cos * q_scale
    q_sin = sin * q_scale
    for hd in range(MLA_HEADS):
        nope = slice(hd * MLA_QK_PAD, hd * MLA_QK_PAD + MLA_NOPE)
        rope = slice(hd * MLA_QK_PAD + MLA_NOPE, (hd + 1) * MLA_QK_PAD)
        q_ref[:, nope] = (qa[:, nope] * q_scale).astype(BF16)
        q_ref[:, rope] = (qa[:, rope] * q_cos + qb[:, hd * LANES:(hd + 1) * LANES] * q_sin).astype(BF16)

    ckvn = _rms(c_kv, gkv_ref[...]).astype(BF16)
    kn = _dot(ckvn, wk_ref[...])
    v_ref[...] = _dot(ckvn, wv_ref[...]).astype(BF16)
    kr = (k_rope * cos + k_rope_rot * sin).astype(BF16)
    for hd in range(MLA_HEADS):
        k_ref[:, hd * MLA_QK_PAD:hd * MLA_QK_PAD + MLA_NOPE] = (
            kn[:, hd * MLA_NOPE:(hd + 1) * MLA_NOPE].astype(BF16))
        k_ref[:, hd * MLA_QK_PAD + MLA_NOPE:(hd + 1) * MLA_QK_PAD] = kr


def _mla_proj(h, w_in, gq, wa, wb, gkv, wk, wv, cos, sin, q_scale, tm):
    t, d = h.shape
    const = lambda i: (0, 0)
    row = lambda i: (i, 0)
    full = lambda a: pl.BlockSpec(a.shape, const)
    qk_w = MLA_HEADS * MLA_QK_PAD
    v_w = MLA_HEADS * MLA_V
    return pl.pallas_call(
        functools.partial(_mla_proj_kernel, q_scale=q_scale),
        grid=(t // tm,),
        in_specs=[
            pl.BlockSpec((tm, d), row), full(w_in), full(gq), full(wa), full(wb),
            full(gkv), full(wk), full(wv),
            pl.BlockSpec((tm, LANES), row), pl.BlockSpec((tm, LANES), row),
        ],
        out_specs=[
            pl.BlockSpec((tm, qk_w), row), pl.BlockSpec((tm, qk_w), row), pl.BlockSpec((tm, v_w), row),
        ],
        out_shape=[
            jax.ShapeDtypeStruct((t, qk_w), BF16), jax.ShapeDtypeStruct((t, qk_w), BF16),
            jax.ShapeDtypeStruct((t, v_w), BF16),
        ],
        compiler_params=_cparams(("parallel",)),
        name="mla_proj",
    )(h, w_in, gq, wa, wb, gkv, wk, wv, cos, sin)


def _hgrn_diag_scores(q, g2, h_ref):
    lane = lax.broadcasted_iota(jnp.int32, (SUBLANES, LANES), 1)
    row1 = lax.broadcasted_iota(jnp.int32, (SUBLANES, 1), 0)
    out = []
    for tv in range(HG_CHUNK // SUBLANES):
        t0 = tv * SUBLANES
        s0 = (t0 // HG_SUB) * HG_SUB
        qv = q[t0:t0 + SUBLANES]
        gv = g2[t0:t0 + SUBLANES]
        acc = jnp.zeros((SUBLANES, LANES), F32)
        for s in range(s0, t0 + SUBLANES):
            m = qv * jnp.exp2(gv - h_ref[s:s + 1, :])
            r = jnp.sum(m, axis=-1, keepdims=True)
            if s >= t0:
                r = jnp.where(row1 + t0 >= s, r, 0.0)
            acc = jnp.where(lane == s, r, acc)
        out.append(acc)
    return jnp.concatenate(out, axis=0)


def _hgrn_kernel(q_ref, f_ref, v_ref, gate_ref, lb_ref, gn_ref, o_ref, st_ref, g_slots, h_slots):
    c = HG_CHUNK
    n_sub = c // HG_SUB
    n_chunks = q_ref.shape[0] // c
    heads = range(HG_HEADS)
    cols = [slice(hd * HG_D, (hd + 1) * HG_D) for hd in heads]

    @pl.when(pl.program_id(1) == 0)
    def _():
        st_ref[...] = jnp.zeros_like(st_ref)

    tri = (lax.broadcasted_iota(jnp.int32, (c, c), 0)
           >= lax.broadcasted_iota(jnp.int32, (c, c), 1)).astype(BF16)
    zeros_sub = jnp.zeros((HG_SUB, LANES), F32)

    def chunk(ci, slot):
        rows = pl.ds(pl.multiple_of(ci * c, c), c)
        g_sc, h_sc = g_slots.at[slot], h_slots.at[slot]

        ks, pieces = [], []
        for cs in cols:
            fp = f_ref[rows, cs]
            lb = lb_ref[:, cs]
            e = jnp.exp(-jnp.abs(fp))
            r = 1.0 / (1.0 + e)
            sig = jnp.where(fp >= 0, r, e * r)
            nsig = jnp.where(fp >= 0, e * r, r)
            log_f = jnp.log(lb + (1.0 - lb) * sig)
            ks.append((1.0 - lb) * nsig)
            p1 = log_f.astype(BF16)
            r1 = log_f - p1.astype(F32)
            p2 = r1.astype(BF16)
            p3 = (r1 - p2.astype(F32)).astype(BF16)
            pieces.append(jnp.concatenate([p1, p2, p3], axis=1))
        gs = []
        for p in pieces:
            r = _dot(tri, p)
            gs.append(r[:, :HG_D] + r[:, HG_D:2 * HG_D] + r[:, 2 * HG_D:])

        qs, vs, g2s, q_in, k_out, q_hat, k_hat = [], [], [], [], [], [], []
        for hd, cs in enumerate(cols):
            g, k = gs[hd], ks[hd]
            g2 = g * LOG2E
            g_sc[hd] = g
            h_sc[hd] = g2 - jnp.log(k) * LOG2E
            g2s.append(g2)
            q = q_ref[rows, cs]
            qs.append(q)
            vs.append(v_ref[rows, cs].astype(BF16))
            g_last = g_sc[hd, c - 1:c, :]
            q_in.append((q * jnp.exp(g)).astype(BF16))
            k_out.append((k * jnp.exp(g_last - g)).astype(BF16))
            qh, kh = [], []
            for sb in range(1, n_sub):
                n = sb * HG_SUB
                g_prev = g_sc[hd, n - 1:n, :]
                qh.append((q[n:n + HG_SUB] * jnp.exp(g[n:n + HG_SUB] - g_prev)).astype(BF16))
                kh.append(jnp.concatenate(
                    [k[:n] * jnp.exp(g_prev - g[:n])] + [zeros_sub] * (n_sub - sb), axis=0).astype(BF16))
            q_hat.append(qh)
            k_hat.append(kh)

        off, inter = [], []
        for hd in heads:
            off.append(jnp.concatenate(
                [zeros_sub[:, :c]] + [_dot_nt(qh, kh) for qh, kh in zip(q_hat[hd], k_hat[hd])], axis=0))
            st = st_ref[hd]
            inter.append(_dot_nt(q_in[hd], st.astype(BF16)))
            g_last = g_sc[hd, c - 1:c, :]
            st_ref[hd] = st * jnp.exp(g_last) + _dot_tn(vs[hd], k_out[hd])

        diag = [_hgrn_diag_scores(qs[hd], g2s[hd], h_sc.at[hd]) for hd in heads]
        outs = [_dot((diag[hd][:, :c] + off[hd]).astype(BF16), vs[hd]) + inter[hd] for hd in heads]

        for hd, cs in enumerate(cols):
            y = _rms(outs[hd], gn_ref[:, cs])
            gate = gate_ref[rows, cs]
            o_ref[rows, cs] = (y * gate * (1.0 / (1.0 + jnp.exp(-gate)))).astype(o_ref.dtype)
        return 0

    lax.fori_loop(0, n_chunks // HG_UNROLL,
                  lambda p, _: sum(chunk(HG_UNROLL * p + u, u) for u in range(HG_UNROLL)), 0)


def _hgrn(proj, lb, gn, batch, tb):
    t = proj.shape[0]
    nt = t // batch // tb
    width = HG_HEADS * HG_D
    col = lambda off: pl.BlockSpec((tb, width), lambda b, i: (b * nt + i, off))
    vec = pl.BlockSpec((1, width), lambda b, i: (0, 0))
    per_head = lambda rows: pltpu.VMEM((HG_HEADS, rows, HG_D), F32)
    slots = pltpu.VMEM((HG_UNROLL, HG_HEADS, HG_CHUNK, HG_D), F32)
    return pl.pallas_call(
        _hgrn_kernel,
        grid=(batch, nt),
        in_specs=[col(0), col(1), col(2), col(3), vec, vec],
        out_specs=pl.BlockSpec((tb, width), lambda b, i: (b * nt + i, 0)),
        out_shape=jax.ShapeDtypeStruct((t, width), BF16),
        scratch_shapes=[per_head(HG_D), slots, slots],
        compiler_params=_cparams(("parallel", "arbitrary")),
        name="hgrn2",
    )(proj, proj, proj, proj, lb, gn)


def _sb_kernel(q_ref, k_ref, v_ref, u_ref, gn_ref, o_ref, later_sc, acc_sc, *, blk, heads):
    i = pl.program_id(2)
    u = u_ref[...]
    row = lax.broadcasted_iota(jnp.int32, (blk, blk), 0)
    col = lax.broadcasted_iota(jnp.int32, (blk, blk), 1)
    strict = col < row
    cols = [slice(hd * SB_D, (hd + 1) * SB_D) for hd in range(heads)]

    def blocks(kb, first):
        rows = pl.ds(pl.multiple_of(kb * blk, blk), blk)
        nzs = [_dot_nt(q_ref[:, c], k_ref[rows, c]) for c in cols]
        stage = []
        for nz in nzs:
            e = jnp.exp2(jnp.abs(nz) * -LOG2E)
            log_keep = jnp.minimum(nz, 0.0) - jnp.log(1.0 + e)
            if first:
                log_keep = jnp.where(strict, log_keep, 0.0)
            stage.append((log_keep.astype(BF16), _row_bcast(jnp.sum(log_keep, axis=-1, keepdims=True))))
        withins = [_dot(lk, u) for lk, _ in stage]
        weights = []
        for hd, (nz, within) in enumerate(zip(nzs, withins)):
            if first:
                a = jnp.where(strict, jnp.exp(within - nz), 0.0)
            else:
                a = jnp.exp(within + _tile_lanes(later_sc[hd], blk) - nz)
            weights.append(a.astype(BF16))
        for hd, (a, c) in enumerate(zip(weights, cols)):
            out = _dot(a, v_ref[rows, c])
            if first:
                later_sc[hd] = stage[hd][1]
                acc_sc[hd] = out
            else:
                later_sc[hd] += stage[hd][1]
                acc_sc[hd] += out

    def step(kb):
        blocks(kb, False)
        return 0

    blocks(i, True)
    odd = i % 2
    top = i - 1 - odd
    lax.fori_loop(0, odd, lambda j, _: step(i - 1), 0)
    lax.fori_loop(0, i // 2, lambda j, _: step(top - 2 * j) + step(top - 2 * j - 1), 0)
    for hd, c in enumerate(cols):
        o_ref[:, c] = _rms(acc_sc[hd], gn_ref[:, c]).astype(o_ref.dtype)


def _sb_attention(qkv, gn, batch, blk, heads):
    t = qkv.shape[0]
    seq = t // batch
    nq = seq // blk
    groups = SB_HEADS // heads
    width = heads * SB_D
    u = (lax.broadcasted_iota(jnp.int32, (blk, blk), 0)
         >= lax.broadcasted_iota(jnp.int32, (blk, blk), 1)).astype(BF16)
    q_spec = pl.BlockSpec((blk, width), lambda b, h, i: (b * nq + i, h))
    k_spec = pl.BlockSpec((seq, width), lambda b, h, i: (b, groups + h))
    v_spec = pl.BlockSpec((seq, width), lambda b, h, i: (b, 2 * groups + h))
    return pl.pallas_call(
        functools.partial(_sb_kernel, blk=blk, heads=heads),
        grid=(batch, groups, nq),
        in_specs=[q_spec, k_spec, v_spec, pl.BlockSpec((blk, blk), lambda b, h, i: (0, 0)),
                  pl.BlockSpec((1, width), lambda b, h, i: (0, h))],
        out_specs=q_spec,
        out_shape=jax.ShapeDtypeStruct((t, SB_HEADS * SB_D), BF16),
        scratch_shapes=[pltpu.VMEM((heads, blk, LANES), F32), pltpu.VMEM((heads, blk, SB_D), F32)],
        compiler_params=_cparams(("parallel", "parallel", "arbitrary")),
        name="sb_attention",
    )(qkv, qkv, qkv, u, gn)


def _mla_kernel(q_ref, k_ref, v_ref, gn_ref, o_ref, m_sc, l_sc, acc_sc, *, blk, heads):
    i = pl.program_id(2)
    row = lax.broadcasted_iota(jnp.int32, (blk, blk), 0)
    col = lax.broadcasted_iota(jnp.int32, (blk, blk), 1)
    allowed = (col // CHUNK) <= (row // CHUNK)

    def blocks(kb, first):
        rows = pl.ds(pl.multiple_of(kb * blk, blk), blk)
        scores = [_dot_nt(q_ref[:, hd * MLA_QK_PAD:(hd + 1) * MLA_QK_PAD],
                          k_ref[rows, hd * MLA_QK_PAD:(hd + 1) * MLA_QK_PAD]) for hd in range(heads)]
        stage = []
        for hd, s in enumerate(scores):
            if first:
                s = jnp.where(allowed, s, -jnp.inf)
                m_new = _row_bcast(jnp.max(s, axis=-1, keepdims=True))
                alpha = None
            else:
                m_old = m_sc[hd]
                m_new = jnp.maximum(m_old, _row_bcast(jnp.max(s, axis=-1, keepdims=True)))
                alpha = jnp.exp2(m_old - m_new)
            p = jnp.exp2(s - _tile_lanes(m_new, blk))
            m_sc[hd] = m_new
            stage.append((alpha, _row_bcast(jnp.sum(p, axis=-1, keepdims=True)), p.astype(BF16)))
        for hd, (alpha, p_sum, p) in enumerate(stage):
            out = _dot(p, v_ref[rows, hd * MLA_V:(hd + 1) * MLA_V])
            if first:
                l_sc[hd] = p_sum
                acc_sc[hd] = out
            else:
                l_sc[hd] = alpha * l_sc[hd] + p_sum
                acc_sc[hd] = alpha * acc_sc[hd] + out

    def step(kb):
        blocks(kb, False)
        return 0

    blocks(i, True)
    odd = i % 2
    lax.fori_loop(0, odd, lambda j, _: step(0), 0)
    lax.fori_loop(0, i // 2, lambda j, _: step(odd + 2 * j) + step(odd + 2 * j + 1), 0)
    for hd in range(heads):
        cols = slice(hd * MLA_V, (hd + 1) * MLA_V)
        o_ref[:, cols] = _rms(acc_sc[hd] / l_sc[hd], gn_ref[:, cols]).astype(o_ref.dtype)


def _mla_attention(q, k, v, gn, batch, blk, heads):
    t = q.shape[0]
    seq = t // batch
    nq = seq // blk
    qk_w = heads * MLA_QK_PAD
    v_w = heads * MLA_V
    return pl.pallas_call(
        functools.partial(_mla_kernel, blk=blk, heads=heads),
        grid=(batch, MLA_HEADS // heads, nq),
        in_specs=[
            pl.BlockSpec((blk, qk_w), lambda b, h, i: (b * nq + i, h)),
            pl.BlockSpec((seq, qk_w), lambda b, h, i: (b, h)),
            pl.BlockSpec((seq, v_w), lambda b, h, i: (b, h)),
            pl.BlockSpec((1, v_w), lambda b, h, i: (0, h)),
        ],
        out_specs=pl.BlockSpec((blk, v_w), lambda b, h, i: (b * nq + i, h)),
        out_shape=jax.ShapeDtypeStruct((t, MLA_HEADS * MLA_V), BF16),
        scratch_shapes=[pltpu.VMEM((heads, blk, LANES), F32), pltpu.VMEM((heads, blk, LANES), F32),
                        pltpu.VMEM((heads, blk, MLA_V), F32)],
        compiler_params=_cparams(("parallel", "parallel", "arbitrary")),
        name="mla_attention",
    )(q, k, v, gn)


def _out_proj_kernel(x_ref, a_ref, b_ref, c_ref, w_ref, o_ref):
    ka, kb = a_ref.shape[1], b_ref.shape[1]
    o_ref[...] = (x_ref[...] + _dot(a_ref[...], w_ref[:ka]) + _dot(b_ref[...], w_ref[ka:ka + kb])
                  + _dot(c_ref[...], w_ref[ka + kb:]))


def _out_proj(x, a, b, c, w, layer, tm, tn):
    t, d = x.shape
    act = lambda arr: pl.BlockSpec((tm, arr.shape[1]), lambda i, j: (i, 0))
    tile = pl.BlockSpec((tm, tn), lambda i, j: (i, j))
    return pl.pallas_call(
        _out_proj_kernel,
        grid=(t // tm, d // tn),
        in_specs=[tile, act(a), act(b), act(c),
                  pl.BlockSpec((None, w.shape[1], tn), lambda i, j: (layer, 0, j))],
        out_specs=tile,
        out_shape=jax.ShapeDtypeStruct((t, d), F32),
        compiler_params=_cparams(("parallel", "parallel")),
        name="out_proj",
    )(x, a, b, c, w)


def _ffn_kernel(x_ref, g_ref, wg_ref, wu_ref, wd_ref, gf_ref, o_ref, h_ref, *, final_norm):
    j = pl.program_id(1)

    @pl.when(j == 0)
    def _():
        x = x_ref[...]
        h_ref[...] = _rms(x, g_ref[...]).astype(BF16)
        o_ref[...] = x

    h = h_ref[...]
    gate = _dot(h, wg_ref[...])
    up = _dot(h, wu_ref[...])
    act = (gate * (1.0 / (1.0 + jnp.exp(-gate))) * up).astype(BF16)
    o_ref[...] += _dot(act, wd_ref[...])

    if final_norm:
        @pl.when(j == pl.num_programs(1) - 1)
        def _():
            o_ref[...] = _rms(o_ref[...], gf_ref[...])


def _ffn(x, g, wg, wu, wd, gf, layer, final_norm, tm, tf):
    t, d = x.shape
    ff = wg.shape[2]
    row = pl.BlockSpec((tm, d), lambda i, j: (i, 0))
    vec = pl.BlockSpec((1, d), lambda i, j: (0, 0))
    return pl.pallas_call(
        functools.partial(_ffn_kernel, final_norm=final_norm),
        grid=(t // tm, ff // tf),
        in_specs=[row, vec, pl.BlockSpec((None, d, tf), lambda i, j: (layer, 0, j)),
                  pl.BlockSpec((None, d, tf), lambda i, j: (layer, 0, j)),
                  pl.BlockSpec((None, tf, d), lambda i, j: (layer, j, 0)), vec],
        out_specs=row,
        out_shape=jax.ShapeDtypeStruct((t, d), F32),
        scratch_shapes=[pltpu.VMEM((tm, d), BF16)],
        compiler_params=_cparams(("parallel", "arbitrary")),
        name="ffn",
    )(x, g, wg, wu, wd, gf)


def _rotate_half_cols(w):
    half = MLA_ROPE // 2
    return jnp.concatenate([-w[..., half:], w[..., :half]], axis=-1)


def _rope_tables(positions):
    inv_freq = ROPE_THETA ** (-jnp.arange(0, MLA_ROPE, 2, dtype=F32) / MLA_ROPE)
    ang = positions.astype(F32).reshape(-1, 1) * inv_freq
    cos2 = jnp.concatenate([jnp.cos(ang)] * 2, axis=-1)
    sin2 = jnp.concatenate([jnp.sin(ang)] * 2, axis=-1)
    pad = jnp.zeros((ang.shape[0], LANES - MLA_ROPE), F32)
    return jnp.concatenate([cos2, pad], axis=-1), jnp.concatenate([sin2, pad], axis=-1)


def _mla_weights(w_in_mla_t, w_uq, w_ukv):
    d = w_in_mla_t.shape[1]
    half = MLA_ROPE // 2
    w_kr = w_in_mla_t[MLA_Q_RANK + MLA_KV_RANK:]
    w_kr_rot = jnp.concatenate([-w_kr[half:], w_kr[:half]], axis=0)
    zpad = jnp.zeros((LANES - MLA_ROPE, d), w_in_mla_t.dtype)
    w_in = jnp.concatenate([w_in_mla_t[:MLA_Q_RANK + MLA_KV_RANK], w_kr, zpad, w_kr_rot, zpad], axis=0)

    uq = w_uq.reshape(MLA_Q_RANK, MLA_HEADS, MLA_NOPE + MLA_ROPE)
    nope, rope = uq[..., :MLA_NOPE], uq[..., MLA_NOPE:]
    zr = jnp.zeros((MLA_Q_RANK, MLA_HEADS, MLA_QK_PAD - MLA_NOPE - MLA_ROPE), F32)
    wa = jnp.concatenate([nope, rope, zr], axis=-1).reshape(MLA_Q_RANK, MLA_HEADS * MLA_QK_PAD)
    wb = jnp.concatenate([_rotate_half_cols(rope), zr], axis=-1).reshape(MLA_Q_RANK, MLA_HEADS * LANES)

    ukv = w_ukv.reshape(MLA_KV_RANK, MLA_HEADS, MLA_NOPE + MLA_V)
    wk = ukv[..., :MLA_NOPE].reshape(MLA_KV_RANK, MLA_HEADS * MLA_NOPE)
    wv = ukv[..., MLA_NOPE:].reshape(MLA_KV_RANK, MLA_HEADS * MLA_V)
    return tuple(w.astype(BF16) for w in (w_in, wa, wb, wk, wv))


def kernel(x, positions, attn_norm_g, w_in, hg_lower_bounds, hg_norm_g, sb_norm_g, mla_q_norm_g,
           mla_w_uq, mla_kv_norm_g, mla_w_ukv, mla_out_norm_g, w_o, ffn_norm_g, w_gate, w_up,
           w_down, final_norm_g):
    batch, seq, d_model = x.shape
    depth = w_in.shape[0]
    t = batch * seq
    hg_w = HG_HEADS * HG_D
    sb_w = SB_HEADS * SB_D
    mla_w = MLA_HEADS * MLA_V
    hg_cols = 4 * hg_w
    sb_cols = 3 * sb_w

    sm = jax.nn.softmax(hg_lower_bounds.astype(F32), axis=0)
    lower_bounds = jnp.cumsum(sm, axis=0) - sm[0:1]
    mla_q_scale = (MLA_NOPE + MLA_ROPE) ** -0.5 * LOG2E
    tables = _rope_tables(positions)

    wo, wg, wu, wd = (w.astype(BF16) for w in (w_o, w_gate, w_up, w_down))
    w_in_t = jnp.swapaxes(w_in, 1, 2).astype(BF16)

    xt = x.reshape(t, d_model)
    for layer in range(depth):
        g_attn = attn_norm_g[layer].reshape(1, d_model)
        w_l = w_in_t[layer]
        w_hg = w_l[:hg_cols]
        w_sb = w_l[hg_cols:hg_cols + sb_cols]
        mla_ws = _mla_weights(w_l[hg_cols + sb_cols:], mla_w_uq[layer], mla_w_ukv[layer])

        hg_proj, h = _norm_proj(xt, g_attn, w_hg, tm=min(t, 512), tn=hg_cols)
        sb_proj = _proj(h, w_sb, -(SB_D ** -0.5), tm=min(t, 1024), tn=sb_w)
        mq, mk, mv = _mla_proj(h, mla_ws[0], mla_q_norm_g[layer].reshape(1, -1), mla_ws[1],
                               mla_ws[2], mla_kv_norm_g[layer].reshape(1, -1), mla_ws[3], mla_ws[4],
                               *tables, mla_q_scale, tm=min(t, 512))

        hg_out = _hgrn(hg_proj, lower_bounds[layer].reshape(1, hg_w), hg_norm_g[layer].reshape(1, hg_w),
                       batch, tb=min(seq, 512))
        sb_out = _sb_attention(sb_proj, sb_norm_g[layer].reshape(1, sb_w), batch, blk=min(seq, 256),
                               heads=ATTN_HEADS_PER_STEP)
        mla_out = _mla_attention(mq, mk, mv, mla_out_norm_g[layer].reshape(1, mla_w), batch,
                                 blk=min(seq, 256), heads=ATTN_HEADS_PER_STEP)

        xt = _out_proj(xt, hg_out, sb_out, mla_out, wo, layer, tm=min(t, 512), tn=d_model)
        xt = _ffn(xt, ffn_norm_g[layer].reshape(1, d_model), wg, wu, wd, final_norm_g.reshape(1, d_model),
                  layer, final_norm=(layer == depth - 1), tm=min(t, 1024), tf=512)
    return xt.reshape(batch, seq, d_model)
```

```python
import functools

import jax
import jax.numpy as jnp
from jax import lax
from jax.experimental import pallas as pl
from jax.experimental.pallas import tpu as pltpu

F32 = jnp.float32
BF16 = jnp.bfloat16

NORM_EPS = 1e-6
LOG2E = 1.4426950408889634
ROPE_THETA = 10000.0
CHUNK = 64

HG_HEADS = 4
HG_D = 128
SB_HEADS = 6
SB_D = 128
MLA_HEADS = 6
MLA_NOPE = 128
MLA_ROPE = 64
MLA_V = 128
MLA_Q_RANK = 512
MLA_KV_RANK = 256
MLA_QK_PAD = 256

ATTN_HEADS_PER_STEP = 6
HG_CHUNK = 64
HG_SUB = 8
HG_UNROLL = 2
SUBLANES = 8
LANES = 128
VMEM_LIMIT = 56 * 1024 * 1024


def _cparams(semantics):
    return pltpu.CompilerParams(dimension_semantics=semantics, vmem_limit_bytes=VMEM_LIMIT)


def _rms(x, g):
    return x * lax.rsqrt(jnp.mean(x * x, axis=-1, keepdims=True) + NORM_EPS) * g


def _dot(a, b):
    return jnp.dot(a, b, preferred_element_type=F32)


def _row_bcast(col):
    return jnp.broadcast_to(col, (col.shape[0], LANES))


def _tile_lanes(x, width):
    return jnp.concatenate([x] * (width // LANES), axis=1)


def _dot_nt(a, b):
    return lax.dot_general(a, b, (((1,), (1,)), ((), ())), preferred_element_type=F32)


def _dot_tn(a, b):
    return lax.dot_general(a, b, (((0,), (0,)), ((), ())), preferred_element_type=F32)


def _in_proj_kernel(x_ref, g_ref, w_ref, hg_ref, sb_ref, h_ref, *, q_cols, q_scale):
    hg_cols = hg_ref.shape[1]
    h = _rms(x_ref[...], g_ref[...]).astype(BF16)
    h_ref[...] = h
    hg_ref[...] = _dot_nt(h, w_ref[:hg_cols])
    sb = _dot_nt(h, w_ref[hg_cols:])
    sb_ref[:, :q_cols] = (sb[:, :q_cols] * q_scale).astype(BF16)
    sb_ref[:, q_cols:] = sb[:, q_cols:].astype(BF16)


def _in_proj(x, g, w_t, layer, hg_cols, sb_cols, q_cols, q_scale, tm):
    t, d = x.shape
    row = lambda width: pl.BlockSpec((tm, width), lambda i: (i, 0))
    return pl.pallas_call(
        functools.partial(_in_proj_kernel, q_cols=q_cols, q_scale=q_scale),
        grid=(t // tm,),
        in_specs=[
            row(d),
            pl.BlockSpec((1, d), lambda i: (0, 0)),
            pl.BlockSpec((None, hg_cols + sb_cols, d), lambda i: (layer, 0, 0), pipeline_mode=pl.Buffered(1)),
        ],
        out_specs=[row(hg_cols), row(sb_cols), row(d)],
        out_shape=[jax.ShapeDtypeStruct((t, hg_cols), F32), jax.ShapeDtypeStruct((t, sb_cols), BF16),
                   jax.ShapeDtypeStruct((t, d), BF16)],
        compiler_params=_cparams(("parallel",)),
        name="in_proj",
    )(x, g, w_t)


def _mla_proj_kernel(h_ref, win_ref, gq_ref, wa_ref, wb_ref, gkv_ref, wk_ref, wv_ref,
                     cos_ref, sin_ref, q_ref, k_ref, v_ref, *, q_scale):
    p = _dot_nt(h_ref[...], win_ref[...])
    c_q = p[:, :MLA_Q_RANK]
    c_kv = p[:, MLA_Q_RANK:MLA_Q_RANK + MLA_KV_RANK]
    k_rope = p[:, MLA_Q_RANK + MLA_KV_RANK:MLA_Q_RANK + MLA_KV_RANK + LANES]
    k_rope_rot = p[:, MLA_Q_RANK + MLA_KV_RANK + LANES:]

    cqn = _rms(c_q, gq_ref[...]).astype(BF16)
    qa = _dot(cqn, wa_ref[...])
    qb = _dot(cqn, wb_ref[...])
    cos = cos_ref[...]
    sin = sin_ref[...]
    q_cos = cos * q_scale
    q_sin = sin * q_scale
    for hd in range(MLA_HEADS):
        nope = slice(hd * MLA_QK_PAD, hd * MLA_QK_PAD + MLA_NOPE)
        rope = slice(hd * MLA_QK_PAD + MLA_NOPE, (hd + 1) * MLA_QK_PAD)
        q_ref[:, nope] = (qa[:, nope] * q_scale).astype(BF16)
        q_ref[:, rope] = (qa[:, rope] * q_cos + qb[:, hd * LANES:(hd + 1) * LANES] * q_sin).astype(BF16)

    ckvn = _rms(c_kv, gkv_ref[...]).astype(BF16)
    kn = _dot(ckvn, wk_ref[...])
    v_ref[...] = _dot(ckvn, wv_ref[...]).astype(BF16)
    kr = (k_rope * cos + k_rope_rot * sin).astype(BF16)
    for hd in range(MLA_HEADS):
        k_ref[:, hd * MLA_QK_PAD:hd * MLA_QK_PAD + MLA_NOPE] = (
            kn[:, hd * MLA_NOPE:(hd + 1) * MLA_NOPE].astype(BF16))
        k_ref[:, hd * MLA_QK_PAD + MLA_NOPE:(hd + 1) * MLA_QK_PAD] = kr


def _mla_proj(h, w_in, gq, wa, wb, gkv, wk, wv, cos, sin, q_scale, tm):
    t, d = h.shape
    const = lambda i: (0, 0)
    row = lambda i: (i, 0)
    full = lambda a: pl.BlockSpec(a.shape, const)
    qk_w = MLA_HEADS * MLA_QK_PAD
    v_w = MLA_HEADS * MLA_V
    return pl.pallas_call(
        functools.partial(_mla_proj_kernel, q_scale=q_scale),
        grid=(t // tm,),
        in_specs=[
            pl.BlockSpec((tm, d), row), full(w_in), full(gq), full(wa), full(wb),
            full(gkv), full(wk), full(wv),
            pl.BlockSpec((tm, LANES), row), pl.BlockSpec((tm, LANES), row),
        ],
        out_specs=[
            pl.BlockSpec((tm, qk_w), row), pl.BlockSpec((tm, qk_w), row), pl.BlockSpec((tm, v_w), row),
        ],
        out_shape=[
            jax.ShapeDtypeStruct((t, qk_w), BF16), jax.ShapeDtypeStruct((t, qk_w), BF16),
            jax.ShapeDtypeStruct((t, v_w), BF16),
        ],
        compiler_params=_cparams(("parallel",)),
        name="mla_proj",
    )(h, w_in, gq, wa, wb, gkv, wk, wv, cos, sin)


def _hgrn_diag_scores(q, g2, h_ref):
    lane = lax.broadcasted_iota(jnp.int32, (SUBLANES, LANES), 1)
    row = lax.broadcasted_iota(jnp.int32, (SUBLANES, LANES), 0)
    out = []
    for tv in range(HG_CHUNK // SUBLANES):
        t0 = tv * SUBLANES
        s0 = (t0 // HG_SUB) * HG_SUB
        qv = q[t0:t0 + SUBLANES]
        gv = g2[t0:t0 + SUBLANES]
        acc = jnp.zeros((SUBLANES, LANES), F32)
        for s in range(s0, t0 + SUBLANES):
            m = qv * jnp.exp2(gv - h_ref[s:s + 1, :])
            acc = jnp.where(lane == s, jnp.sum(m, axis=-1, keepdims=True), acc)
        out.append(jnp.where(lane <= row + t0, acc, 0.0))
    return jnp.concatenate(out, axis=0)


def _hgrn_kernel(q_ref, f_ref, v_ref, gate_ref, lb_ref, gn_ref, o_ref, st_ref, g_slots, h_slots):
    c = HG_CHUNK
    n_sub = c // HG_SUB
    n_chunks = q_ref.shape[0] // c
    heads = range(HG_HEADS)
    cols = [slice(hd * HG_D, (hd + 1) * HG_D) for hd in heads]

    @pl.when(pl.program_id(1) == 0)
    def _():
        st_ref[...] = jnp.zeros_like(st_ref)

    tri3 = (lax.broadcasted_iota(jnp.int32, (c, 3 * c), 0)
            >= lax.broadcasted_iota(jnp.int32, (c, 3 * c), 1) % c).astype(BF16)
    zeros_sub = jnp.zeros((HG_SUB, LANES), F32)

    def chunk(ci, slot):
        rows = pl.ds(pl.multiple_of(ci * c, c), c)
        g_sc, h_sc = g_slots.at[slot], h_slots.at[slot]

        ks, pieces = [], []
        for cs in cols:
            fp = f_ref[rows, cs]
            lb = lb_ref[:, cs]
            e = jnp.exp(-jnp.abs(fp))
            r = 1.0 / (1.0 + e)
            sig = jnp.where(fp >= 0, r, e * r)
            nsig = jnp.where(fp >= 0, e * r, r)
            log_f = jnp.log(lb + (1.0 - lb) * sig)
            ks.append((1.0 - lb) * nsig)
            p1 = log_f.astype(BF16)
            r1 = log_f - p1.astype(F32)
            p2 = r1.astype(BF16)
            p3 = (r1 - p2.astype(F32)).astype(BF16)
            pieces.append(jnp.concatenate([p1, p2, p3], axis=0))
        gs = [_dot(tri3, p) for p in pieces]

        qs, vs, g2s, q_in, k_out, q_hat, k_hat = [], [], [], [], [], [], []
        for hd, cs in enumerate(cols):
            g, k = gs[hd], ks[hd]
            g2 = g * LOG2E
            g_sc[hd] = g
            h_sc[hd] = g2 - jnp.log(k) * LOG2E
            g2s.append(g2)
            q = q_ref[rows, cs]
            qs.append(q)
            vs.append(v_ref[rows, cs].astype(BF16))
            g_last = g_sc[hd, c - 1:c, :]
            q_in.append((q * jnp.exp(g)).astype(BF16))
            k_out.append((k * jnp.exp(g_last - g)).astype(BF16))
            qh, kh = [], []
            for sb in range(1, n_sub):
                n = sb * HG_SUB
                g_prev = g_sc[hd, n - 1:n, :]
                qh.append((q[n:n + HG_SUB] * jnp.exp(g[n:n + HG_SUB] - g_prev)).astype(BF16))
                kh.append(jnp.concatenate(
                    [k[:n] * jnp.exp(g_prev - g[:n])] + [zeros_sub] * (n_sub - sb), axis=0).astype(BF16))
            q_hat.append(qh)
            k_hat.append(kh)

        off, inter = [], []
        for hd in heads:
            off.append(jnp.concatenate(
                [zeros_sub[:, :c]] + [_dot_nt(qh, kh) for qh, kh in zip(q_hat[hd], k_hat[hd])], axis=0))
            st = st_ref[hd]
            inter.append(_dot_nt(q_in[hd], st.astype(BF16)))
            g_last = g_sc[hd, c - 1:c, :]
            st_ref[hd] = st * jnp.exp(g_last) + _dot_tn(vs[hd], k_out[hd])

        diag = [_hgrn_diag_scores(qs[hd], g2s[hd], h_sc.at[hd]) for hd in heads]
        outs = [_dot((diag[hd][:, :c] + off[hd]).astype(BF16), vs[hd]) + inter[hd] for hd in heads]

        for hd, cs in enumerate(cols):
            y = _rms(outs[hd], gn_ref[:, cs])
            gate = gate_ref[rows, cs]
            o_ref[rows, cs] = (y * gate * (1.0 / (1.0 + jnp.exp(-gate)))).astype(o_ref.dtype)
        return 0

    lax.fori_loop(0, n_chunks // HG_UNROLL,
                  lambda p, _: sum(chunk(HG_UNROLL * p + u, u) for u in range(HG_UNROLL)), 0)


def _hgrn(proj, lb, gn, batch, tb):
    t = proj.shape[0]
    nt = t // batch // tb
    width = HG_HEADS * HG_D
    col = lambda off: pl.BlockSpec((tb, width), lambda b, i: (b * nt + i, off))
    vec = pl.BlockSpec((1, width), lambda b, i: (0, 0))
    per_head = lambda rows: pltpu.VMEM((HG_HEADS, rows, HG_D), F32)
    slots = pltpu.VMEM((HG_UNROLL, HG_HEADS, HG_CHUNK, HG_D), F32)
    return pl.pallas_call(
        _hgrn_kernel,
        grid=(batch, nt),
        in_specs=[col(0), col(1), col(2), col(3), vec, vec],
        out_specs=pl.BlockSpec((tb, width), lambda b, i: (b * nt + i, 0)),
        out_shape=jax.ShapeDtypeStruct((t, width), BF16),
        scratch_shapes=[per_head(HG_D), slots, slots],
        compiler_params=_cparams(("parallel", "arbitrary")),
        name="hgrn2",
    )(proj, proj, proj, proj, lb, gn)


def _sb_kernel(q_ref, k_ref, v_ref, u_ref, gn_ref, o_ref, later_sc, acc_sc, *, blk, heads):
    i = pl.program_id(2)
    u = u_ref[...]
    row = lax.broadcasted_iota(jnp.int32, (blk, blk), 0)
    col = lax.broadcasted_iota(jnp.int32, (blk, blk), 1)
    strict = col < row
    cols = [slice(hd * SB_D, (hd + 1) * SB_D) for hd in range(heads)]

    def blocks(kb, first):
        rows = pl.ds(pl.multiple_of(kb * blk, blk), blk)
        nzs = [_dot_nt(q_ref[:, c], k_ref[rows, c]) for c in cols]
        stage = []
        for nz in nzs:
            e = jnp.exp2(jnp.abs(nz) * -LOG2E)
            log_keep = jnp.minimum(nz, 0.0) - jnp.log(1.0 + e)
            if first:
                log_keep = jnp.where(strict, log_keep, 0.0)
            stage.append((log_keep.astype(BF16), _row_bcast(jnp.sum(log_keep, axis=-1, keepdims=True))))
        withins = [_dot(lk, u) for lk, _ in stage]
        weights = []
        for hd, (nz, within) in enumerate(zip(nzs, withins)):
            if first:
                a = jnp.where(strict, jnp.exp(within - nz), 0.0)
            else:
                a = jnp.exp(within + _tile_lanes(later_sc[hd], blk) - nz)
            weights.append(a.astype(BF16))
        for hd, (a, c) in enumerate(zip(weights, cols)):
            out = _dot(a, v_ref[rows, c])
            if first:
                later_sc[hd] = stage[hd][1]
                acc_sc[hd] = out
            else:
                later_sc[hd] += stage[hd][1]
                acc_sc[hd] += out

    def step(kb):
        blocks(kb, False)
        return 0

    blocks(i, True)
    odd = i % 2
    top = i - 1 - odd
    lax.fori_loop(0, odd, lambda j, _: step(i - 1), 0)
    lax.fori_loop(0, i // 2, lambda j, _: step(top - 2 * j) + step(top - 2 * j - 1), 0)
    for hd, c in enumerate(cols):
        o_ref[:, c] = _rms(acc_sc[hd], gn_ref[:, c]).astype(o_ref.dtype)


def _sb_attention(qkv, gn, batch, blk, heads):
    t = qkv.shape[0]
    seq = t // batch
    nq = seq // blk
    groups = SB_HEADS // heads
    width = heads * SB_D
    u = (lax.broadcasted_iota(jnp.int32, (blk, blk), 0)
         >= lax.broadcasted_iota(jnp.int32, (blk, blk), 1)).astype(BF16)
    q_spec = pl.BlockSpec((blk, width), lambda b, h, i: (b * nq + i, h))
    k_spec = pl.BlockSpec((seq, width), lambda b, h, i: (b, groups + h))
    v_spec = pl.BlockSpec((seq, width), lambda b, h, i: (b, 2 * groups + h))
    return pl.pallas_call(
        functools.partial(_sb_kernel, blk=blk, heads=heads),
        grid=(batch, groups, nq),
        in_specs=[q_spec, k_spec, v_spec, pl.BlockSpec((blk, blk), lambda b, h, i: (0, 0)),
                  pl.BlockSpec((1, width), lambda b, h, i: (0, h))],
        out_specs=q_spec,
        out_shape=jax.ShapeDtypeStruct((t, SB_HEADS * SB_D), BF16),
        scratch_shapes=[pltpu.VMEM((heads, blk, LANES), F32), pltpu.VMEM((heads, blk, SB_D), F32)],
        compiler_params=_cparams(("parallel", "parallel", "arbitrary")),
        name="sb_attention",
    )(qkv, qkv, qkv, u, gn)


def _mla_kernel(q_ref, k_ref, v_ref, gn_ref, o_ref, m_sc, l_sc, acc_sc, *, blk, heads):
    i = pl.program_id(2)
    row = lax.broadcasted_iota(jnp.int32, (blk, blk), 0)
    col = lax.broadcasted_iota(jnp.int32, (blk, blk), 1)
    allowed = (col // CHUNK) <= (row // CHUNK)

    def blocks(kb, first):
        rows = pl.ds(pl.multiple_of(kb * blk, blk), blk)
        scores = [_dot_nt(q_ref[:, hd * MLA_QK_PAD:(hd + 1) * MLA_QK_PAD],
                          k_ref[rows, hd * MLA_QK_PAD:(hd + 1) * MLA_QK_PAD]) for hd in range(heads)]
        stage = []
        for hd, s in enumerate(scores):
            if first:
                s = jnp.where(allowed, s, -jnp.inf)
                m_new = _row_bcast(jnp.max(s, axis=-1, keepdims=True))
                alpha = None
            else:
                m_old = m_sc[hd]
                m_new = jnp.maximum(m_old, _row_bcast(jnp.max(s, axis=-1, keepdims=True)))
                alpha = jnp.exp2(m_old - m_new)
            p = jnp.exp2(s - _tile_lanes(m_new, blk))
            m_sc[hd] = m_new
            stage.append((alpha, _row_bcast(jnp.sum(p, axis=-1, keepdims=True)), p.astype(BF16)))
        for hd, (alpha, p_sum, p) in enumerate(stage):
            out = _dot(p, v_ref[rows, hd * MLA_V:(hd + 1) * MLA_V])
            if first:
                l_sc[hd] = p_sum
                acc_sc[hd] = out
            else:
                l_sc[hd] = alpha * l_sc[hd] + p_sum
                acc_sc[hd] = alpha * acc_sc[hd] + out

    def step(kb):
        blocks(kb, False)
        return 0

    blocks(i, True)
    odd = i % 2
    lax.fori_loop(0, odd, lambda j, _: step(0), 0)
    lax.fori_loop(0, i // 2, lambda j, _: step(odd + 2 * j) + step(odd + 2 * j + 1), 0)
    for hd in range(heads):
        cols = slice(hd * MLA_V, (hd + 1) * MLA_V)
        o_ref[:, cols] = _rms(acc_sc[hd] / l_sc[hd], gn_ref[:, cols]).astype(o_ref.dtype)


def _mla_attention(q, k, v, gn, batch, blk, heads):
    t = q.shape[0]
    seq = t // batch
    nq = seq // blk
    qk_w = heads * MLA_QK_PAD
    v_w = heads * MLA_V
    return pl.pallas_call(
        functools.partial(_mla_kernel, blk=blk, heads=heads),
        grid=(batch, MLA_HEADS // heads, nq),
        in_specs=[
            pl.BlockSpec((blk, qk_w), lambda b, h, i: (b * nq + i, h)),
            pl.BlockSpec((seq, qk_w), lambda b, h, i: (b, h)),
            pl.BlockSpec((seq, v_w), lambda b, h, i: (b, h)),
            pl.BlockSpec((1, v_w), lambda b, h, i: (0, h)),
        ],
        out_specs=pl.BlockSpec((blk, v_w), lambda b, h, i: (b * nq + i, h)),
        out_shape=jax.ShapeDtypeStruct((t, MLA_HEADS * MLA_V), BF16),
        scratch_shapes=[pltpu.VMEM((heads, blk, LANES), F32), pltpu.VMEM((heads, blk, LANES), F32),
                        pltpu.VMEM((heads, blk, MLA_V), F32)],
        compiler_params=_cparams(("parallel", "parallel", "arbitrary")),
        name="mla_attention",
    )(q, k, v, gn)


def _out_proj_kernel(x_ref, a_ref, b_ref, c_ref, w_ref, o_ref):
    ka, kb = a_ref.shape[1], b_ref.shape[1]
    o_ref[...] = (x_ref[...] + _dot(a_ref[...], w_ref[:ka]) + _dot(b_ref[...], w_ref[ka:ka + kb])
                  + _dot(c_ref[...], w_ref[ka + kb:]))


def _out_proj(x, a, b, c, w, layer, tm, tn):
    t, d = x.shape
    act = lambda arr: pl.BlockSpec((tm, arr.shape[1]), lambda i, j: (i, 0))
    tile = pl.BlockSpec((tm, tn), lambda i, j: (i, j))
    return pl.pallas_call(
        _out_proj_kernel,
        grid=(t // tm, d // tn),
        in_specs=[tile, act(a), act(b), act(c),
                  pl.BlockSpec((None, w.shape[1], tn), lambda i, j: (layer, 0, j))],
        out_specs=tile,
        out_shape=jax.ShapeDtypeStruct((t, d), F32),
        compiler_params=_cparams(("parallel", "parallel")),
        name="out_proj",
    )(x, a, b, c, w)


def _ffn_kernel(x_ref, g_ref, wg_ref, wu_ref, wd_ref, gf_ref, o_ref, h_ref, *, final_norm):
    j = pl.program_id(1)

    @pl.when(j == 0)
    def _():
        x = x_ref[...]
        h_ref[...] = _rms(x, g_ref[...]).astype(BF16)
        o_ref[...] = x

    h = h_ref[...]
    gate = _dot(h, wg_ref[...])
    up = _dot(h, wu_ref[...])
    act = (gate * (1.0 / (1.0 + jnp.exp(-gate))) * up).astype(BF16)
    o_ref[...] += _dot(act, wd_ref[...])

    if final_norm:
        @pl.when(j == pl.num_programs(1) - 1)
        def _():
            o_ref[...] = _rms(o_ref[...], gf_ref[...])


def _ffn(x, g, wg, wu, wd, gf, layer, final_norm, tm, tf):
    t, d = x.shape
    ff = wg.shape[2]
    row = pl.BlockSpec((tm, d), lambda i, j: (i, 0))
    vec = pl.BlockSpec((1, d), lambda i, j: (0, 0))
    return pl.pallas_call(
        functools.partial(_ffn_kernel, final_norm=final_norm),
        grid=(t // tm, ff // tf),
        in_specs=[row, vec, pl.BlockSpec((None, d, tf), lambda i, j: (layer, 0, j)),
                  pl.BlockSpec((None, d, tf), lambda i, j: (layer, 0, j)),
                  pl.BlockSpec((None, tf, d), lambda i, j: (layer, j, 0)), vec],
        out_specs=row,
        out_shape=jax.ShapeDtypeStruct((t, d), F32),
        scratch_shapes=[pltpu.VMEM((tm, d), BF16)],
        compiler_params=_cparams(("parallel", "arbitrary")),
        name="ffn",
    )(x, g, wg, wu, wd, gf)


def _rotate_half_cols(w):
    half = MLA_ROPE // 2
    return jnp.concatenate([-w[..., half:], w[..., :half]], axis=-1)


def _rope_tables(positions):
    inv_freq = ROPE_THETA ** (-jnp.arange(0, MLA_ROPE, 2, dtype=F32) / MLA_ROPE)
    ang = positions.astype(F32).reshape(-1, 1) * inv_freq
    cos2 = jnp.concatenate([jnp.cos(ang)] * 2, axis=-1)
    sin2 = jnp.concatenate([jnp.sin(ang)] * 2, axis=-1)
    pad = jnp.zeros((ang.shape[0], LANES - MLA_ROPE), F32)
    return jnp.concatenate([cos2, pad], axis=-1), jnp.concatenate([sin2, pad], axis=-1)


def _mla_weights(w_in_mla_t, w_uq, w_ukv):
    d = w_in_mla_t.shape[1]
    half = MLA_ROPE // 2
    w_kr = w_in_mla_t[MLA_Q_RANK + MLA_KV_RANK:]
    w_kr_rot = jnp.concatenate([-w_kr[half:], w_kr[:half]], axis=0)
    zpad = jnp.zeros((LANES - MLA_ROPE, d), w_in_mla_t.dtype)
    w_in = jnp.concatenate([w_in_mla_t[:MLA_Q_RANK + MLA_KV_RANK], w_kr, zpad, w_kr_rot, zpad], axis=0)

    uq = w_uq.reshape(MLA_Q_RANK, MLA_HEADS, MLA_NOPE + MLA_ROPE)
    nope, rope = uq[..., :MLA_NOPE], uq[..., MLA_NOPE:]
    zr = jnp.zeros((MLA_Q_RANK, MLA_HEADS, MLA_QK_PAD - MLA_NOPE - MLA_ROPE), F32)
    wa = jnp.concatenate([nope, rope, zr], axis=-1).reshape(MLA_Q_RANK, MLA_HEADS * MLA_QK_PAD)
    wb = jnp.concatenate([_rotate_half_cols(rope), zr], axis=-1).reshape(MLA_Q_RANK, MLA_HEADS * LANES)

    ukv = w_ukv.reshape(MLA_KV_RANK, MLA_HEADS, MLA_NOPE + MLA_V)
    wk = ukv[..., :MLA_NOPE].reshape(MLA_KV_RANK, MLA_HEADS * MLA_NOPE)
    wv = ukv[..., MLA_NOPE:].reshape(MLA_KV_RANK, MLA_HEADS * MLA_V)
    return tuple(w.astype(BF16) for w in (w_in, wa, wb, wk, wv))


def kernel(x, positions, attn_norm_g, w_in, hg_lower_bounds, hg_norm_g, sb_norm_g, mla_q_norm_g,
           mla_w_uq, mla_kv_norm_g, mla_w_ukv, mla_out_norm_g, w_o, ffn_norm_g, w_gate, w_up,
           w_down, final_norm_g):
    batch, seq, d_model = x.shape
    depth = w_in.shape[0]
    t = batch * seq
    hg_w = HG_HEADS * HG_D
    sb_w = SB_HEADS * SB_D
    mla_w = MLA_HEADS * MLA_V
    hg_cols = 4 * hg_w
    sb_cols = 3 * sb_w

    sm = jax.nn.softmax(hg_lower_bounds.astype(F32), axis=0)
    lower_bounds = jnp.cumsum(sm, axis=0) - sm[0:1]
    mla_q_scale = (MLA_NOPE + MLA_ROPE) ** -0.5 * LOG2E
    tables = _rope_tables(positions)

    wo, wg, wu, wd = (w.astype(BF16) for w in (w_o, w_gate, w_up, w_down))
    w_in_t = jnp.swapaxes(w_in, 1, 2).astype(BF16)

    xt = x.reshape(t, d_model)
    for layer in range(depth):
        g_attn = attn_norm_g[layer].reshape(1, d_model)
        mla_ws = _mla_weights(w_in_t[layer, hg_cols + sb_cols:], mla_w_uq[layer], mla_w_ukv[layer])

        hg_proj, sb_proj, h = _in_proj(xt, g_attn, w_in_t, layer, hg_cols, sb_cols, sb_w, -(SB_D ** -0.5),
                                       tm=min(t, 512))
        mq, mk, mv = _mla_proj(h, mla_ws[0], mla_q_norm_g[layer].reshape(1, -1), mla_ws[1],
                               mla_ws[2], mla_kv_norm_g[layer].reshape(1, -1), mla_ws[3], mla_ws[4],
                               *tables, mla_q_scale, tm=min(t, 512))

        hg_out = _hgrn(hg_proj, lower_bounds[layer].reshape(1, hg_w), hg_norm_g[layer].reshape(1, hg_w),
                       batch, tb=min(seq, 512))
        sb_out = _sb_attention(sb_proj, sb_norm_g[layer].reshape(1, sb_w), batch, blk=min(seq, 256),
                               heads=ATTN_HEADS_PER_STEP)
        mla_out = _mla_attention(mq, mk, mv, mla_out_norm_g[layer].reshape(1, mla_w), batch,
                                 blk=min(seq, 256), heads=ATTN_HEADS_PER_STEP)

        xt = _out_proj(xt, hg_out, sb_out, mla_out, wo, layer, tm=min(t, 512), tn=d_model)
        xt = _ffn(xt, ffn_norm_g[layer].reshape(1, d_model), wg, wu, wd, final_norm_g.reshape(1, d_model),
                  layer, final_norm=(layer == depth - 1), tm=min(t, 1024), tf=512)
    return xt.reshape(batch, seq, d_model)
```

```python
import functools

import jax
import jax.numpy as jnp
from jax import lax
from jax.experimental import pallas as pl
from jax.experimental.pallas import tpu as pltpu

F32 = jnp.float32
BF16 = jnp.bfloat16

NORM_EPS = 1e-6
LOG2E = 1.4426950408889634
ROPE_THETA = 10000.0
CHUNK = 64

HG_HEADS = 4
HG_D = 128
SB_HEADS = 6
SB_D = 128
MLA_HEADS = 6
MLA_NOPE = 128
MLA_ROPE = 64
MLA_V = 128
MLA_Q_RANK = 512
MLA_KV_RANK = 256
MLA_QK_PAD = 256

ATTN_HEADS_PER_STEP = 6
HG_CHUNK = 64
HG_SUB = 8
HG_UNROLL = 2
SUBLANES = 8
LANES = 128
VMEM_LIMIT = 56 * 1024 * 1024


def _cparams(semantics):
    return pltpu.CompilerParams(dimension_semantics=semantics, vmem_limit_bytes=VMEM_LIMIT)


def _rms(x, g):
    return x * lax.rsqrt(jnp.mean(x * x, axis=-1, keepdims=True) + NORM_EPS) * g


def _dot(a, b):
    return jnp.dot(a, b, preferred_element_type=F32)


def _row_bcast(col):
    return jnp.broadcast_to(col, (col.shape[0], LANES))


def _tile_lanes(x, width):
    return jnp.concatenate([x] * (width // LANES), axis=1)


def _dot_nt(a, b):
    return lax.dot_general(a, b, (((1,), (1,)), ((), ())), preferred_element_type=F32)


def _dot_tn(a, b):
    return lax.dot_general(a, b, (((0,), (0,)), ((), ())), preferred_element_type=F32)


def _in_proj_kernel(x_ref, g_ref, w_ref, hg_ref, sb_ref, h_ref, *, q_cols, q_scale):
    hg_cols = hg_ref.shape[1]
    h = _rms(x_ref[...], g_ref[...]).astype(BF16)
    h_ref[...] = h
    hg_ref[...] = _dot_nt(h, w_ref[:hg_cols])
    sb = _dot_nt(h, w_ref[hg_cols:])
    sb_ref[:, :q_cols] = (sb[:, :q_cols] * q_scale).astype(BF16)
    sb_ref[:, q_cols:] = sb[:, q_cols:].astype(BF16)


def _in_proj(x, g, w_t, layer, hg_cols, sb_cols, q_cols, q_scale, tm):
    t, d = x.shape
    row = lambda width: pl.BlockSpec((tm, width), lambda i: (i, 0))
    return pl.pallas_call(
        functools.partial(_in_proj_kernel, q_cols=q_cols, q_scale=q_scale),
        grid=(t // tm,),
        in_specs=[
            row(d),
            pl.BlockSpec((1, d), lambda i: (0, 0)),
            pl.BlockSpec((None, hg_cols + sb_cols, d), lambda i: (layer, 0, 0), pipeline_mode=pl.Buffered(1)),
        ],
        out_specs=[row(hg_cols), row(sb_cols), row(d)],
        out_shape=[jax.ShapeDtypeStruct((t, hg_cols), F32), jax.ShapeDtypeStruct((t, sb_cols), BF16),
                   jax.ShapeDtypeStruct((t, d), BF16)],
        compiler_params=_cparams(("parallel",)),
        name="in_proj",
    )(x, g, w_t)


def _mla_proj_kernel(h_ref, win_ref, gq_ref, wa_ref, wb_ref, gkv_ref, wk_ref, wv_ref,
                     cos_ref, sin_ref, q_ref, k_ref, v_ref, *, q_scale):
    p = _dot_nt(h_ref[...], win_ref[...])
    c_q = p[:, :MLA_Q_RANK]
    c_kv = p[:, MLA_Q_RANK:MLA_Q_RANK + MLA_KV_RANK]
    k_rope = p[:, MLA_Q_RANK + MLA_KV_RANK:MLA_Q_RANK + MLA_KV_RANK + LANES]
    k_rope_rot = p[:, MLA_Q_RANK + MLA_KV_RANK + LANES:]

    cqn = _rms(c_q, gq_ref[...]).astype(BF16)
    qa = _dot(cqn, wa_ref[...])
    qb = _dot(cqn, wb_ref[...])
    cos = cos_ref[...]
    sin = sin_ref[...]
    q_cos = cos * q_scale
    q_sin = sin * q_scale
    for hd in range(MLA_HEADS):
        nope = slice(hd * MLA_QK_PAD, hd * MLA_QK_PAD + MLA_NOPE)
        rope = slice(hd * MLA_QK_PAD + MLA_NOPE, (hd + 1) * MLA_QK_PAD)
        q_ref[:, nope] = (qa[:, nope] * q_scale).astype(BF16)
        q_ref[:, rope] = (qa[:, rope] * q_cos + qb[:, hd * LANES:(hd + 1) * LANES] * q_sin).astype(BF16)

    ckvn = _rms(c_kv, gkv_ref[...]).astype(BF16)
    kn = _dot(ckvn, wk_ref[...])
    v_ref[...] = _dot(ckvn, wv_ref[...]).astype(BF16)
    kr = (k_rope * cos + k_rope_rot * sin).astype(BF16)
    for hd in range(MLA_HEADS):
        k_ref[:, hd * MLA_QK_PAD:hd * MLA_QK_PAD + MLA_NOPE] = (
            kn[:, hd * MLA_NOPE:(hd + 1) * MLA_NOPE].astype(BF16))
        k_ref[:, hd * MLA_QK_PAD + MLA_NOPE:(hd + 1) * MLA_QK_PAD] = kr


def _mla_proj(h, w_in, gq, wa, wb, gkv, wk, wv, cos, sin, q_scale, tm):
    t, d = h.shape
    const = lambda i: (0, 0)
    row = lambda i: (i, 0)
    full = lambda a: pl.BlockSpec(a.shape, const)
    qk_w = MLA_HEADS * MLA_QK_PAD
    v_w = MLA_HEADS * MLA_V
    return pl.pallas_call(
        functools.partial(_mla_proj_kernel, q_scale=q_scale),
        grid=(t // tm,),
        in_specs=[
            pl.BlockSpec((tm, d), row), full(w_in), full(gq), full(wa), full(wb),
            full(gkv), full(wk), full(wv),
            pl.BlockSpec((tm, LANES), row), pl.BlockSpec((tm, LANES), row),
        ],
        out_specs=[
            pl.BlockSpec((tm, qk_w), row), pl.BlockSpec((tm, qk_w), row), pl.BlockSpec((tm, v_w), row),
        ],
        out_shape=[
            jax.ShapeDtypeStruct((t, qk_w), BF16), jax.ShapeDtypeStruct((t, qk_w), BF16),
            jax.ShapeDtypeStruct((t, v_w), BF16),
        ],
        compiler_params=_cparams(("parallel",)),
        name="mla_proj",
    )(h, w_in, gq, wa, wb, gkv, wk, wv, cos, sin)


def _hgrn_diag_scores(q, g2, h_ref):
    lane = lax.broadcasted_iota(jnp.int32, (SUBLANES, LANES), 1)
    row = lax.broadcasted_iota(jnp.int32, (SUBLANES, LANES), 0)
    out = []
    for tv in range(HG_CHUNK // SUBLANES):
        t0 = tv * SUBLANES
        s0 = (t0 // HG_SUB) * HG_SUB
        qv = q[t0:t0 + SUBLANES]
        gv = g2[t0:t0 + SUBLANES]
        acc = jnp.zeros((SUBLANES, LANES), F32)
        for s in range(s0, t0 + SUBLANES):
            m = qv * jnp.exp2(gv - h_ref[s:s + 1, :])
            acc = jnp.where(lane == s, jnp.sum(m, axis=-1, keepdims=True), acc)
        out.append(jnp.where(lane <= row + t0, acc, 0.0))
    return jnp.concatenate(out, axis=0)


def _hgrn_kernel(q_ref, f_ref, v_ref, gate_ref, lb_ref, gn_ref, o_ref, st_ref, g_slots, h_slots):
    c = HG_CHUNK
    n_sub = c // HG_SUB
    n_chunks = q_ref.shape[0] // c
    heads = range(HG_HEADS)
    cols = [slice(hd * HG_D, (hd + 1) * HG_D) for hd in heads]

    @pl.when(pl.program_id(1) == 0)
    def _():
        st_ref[...] = jnp.zeros_like(st_ref)

    tri3 = (lax.broadcasted_iota(jnp.int32, (c, 3 * c), 0)
            >= lax.broadcasted_iota(jnp.int32, (c, 3 * c), 1) % c).astype(BF16)
    zeros_sub = jnp.zeros((HG_SUB, LANES), F32)

    def chunk(ci, slot):
        rows = pl.ds(pl.multiple_of(ci * c, c), c)
        g_sc, h_sc = g_slots.at[slot], h_slots.at[slot]

        ks, pieces = [], []
        for cs in cols:
            fp = f_ref[rows, cs]
            lb = lb_ref[:, cs]
            e = jnp.exp(-jnp.abs(fp))
            r = 1.0 / (1.0 + e)
            sig = jnp.where(fp >= 0, r, e * r)
            nsig = jnp.where(fp >= 0, e * r, r)
            log_f = jnp.log(lb + (1.0 - lb) * sig)
            ks.append((1.0 - lb) * nsig)
            p1 = log_f.astype(BF16)
            r1 = log_f - p1.astype(F32)
            p2 = r1.astype(BF16)
            p3 = (r1 - p2.astype(F32)).astype(BF16)
            pieces.append(jnp.concatenate([p1, p2, p3], axis=0))
        gs = [_dot(tri3, p) for p in pieces]

        qs, vs, g2s, q_in, k_out, q_hat, k_hat = [], [], [], [], [], [], []
        for hd, cs in enumerate(cols):
            g, k = gs[hd], ks[hd]
            g2 = g * LOG2E
            g_sc[hd] = g
            h_sc[hd] = g2 - jnp.log(k) * LOG2E
            g2s.append(g2)
            q = q_ref[rows, cs]
            qs.append(q)
            vs.append(v_ref[rows, cs].astype(BF16))
            g_last = g_sc[hd, c - 1:c, :]
            q_in.append((q * jnp.exp(g)).astype(BF16))
            k_out.append((k * jnp.exp(g_last - g)).astype(BF16))
            qh, kh = [], []
            for sb in range(1, n_sub):
                n = sb * HG_SUB
                g_prev = g_sc[hd, n - 1:n, :]
                qh.append((q[n:n + HG_SUB] * jnp.exp(g[n:n + HG_SUB] - g_prev)).astype(BF16))
                kh.append(jnp.concatenate(
                    [k[:n] * jnp.exp(g_prev - g[:n])] + [zeros_sub] * (n_sub - sb), axis=0).astype(BF16))
            q_hat.append(qh)
            k_hat.append(kh)

        off, inter = [], []
        for hd in heads:
            off.append(jnp.concatenate(
                [zeros_sub[:, :c]] + [_dot_nt(qh, kh) for qh, kh in zip(q_hat[hd], k_hat[hd])], axis=0))
            st = st_ref[hd]
            inter.append(_dot_nt(q_in[hd], st.astype(BF16)))
            g_last = g_sc[hd, c - 1:c, :]
            st_ref[hd] = st * jnp.exp(g_last) + _dot_tn(vs[hd], k_out[hd])

        diag = [_hgrn_diag_scores(qs[hd], g2s[hd], h_sc.at[hd]) for hd in heads]
        outs = [_dot((diag[hd][:, :c] + off[hd]).astype(BF16), vs[hd]) + inter[hd] for hd in heads]

        for hd, cs in enumerate(cols):
            y = _rms(outs[hd], gn_ref[:, cs])
            gate = gate_ref[rows, cs]
            o_ref[rows, cs] = (y * gate * (1.0 / (1.0 + jnp.exp(-gate)))).astype(o_ref.dtype)
        return 0

    lax.fori_loop(0, n_chunks // HG_UNROLL,
                  lambda p, _: sum(chunk(HG_UNROLL * p + u, u) for u in range(HG_UNROLL)), 0)


def _hgrn(proj, lb, gn, batch, tb):
    t = proj.shape[0]
    nt = t // batch // tb
    width = HG_HEADS * HG_D
    col = lambda off: pl.BlockSpec((tb, width), lambda b, i: (b * nt + i, off))
    vec = pl.BlockSpec((1, width), lambda b, i: (0, 0))
    per_head = lambda rows: pltpu.VMEM((HG_HEADS, rows, HG_D), F32)
    slots = pltpu.VMEM((HG_UNROLL, HG_HEADS, HG_CHUNK, HG_D), F32)
    return pl.pallas_call(
        _hgrn_kernel,
        grid=(batch, nt),
        in_specs=[col(0), col(1), col(2), col(3), vec, vec],
        out_specs=pl.BlockSpec((tb, width), lambda b, i: (b * nt + i, 0)),
        out_shape=jax.ShapeDtypeStruct((t, width), BF16),
        scratch_shapes=[per_head(HG_D), slots, slots],
        compiler_params=_cparams(("parallel", "arbitrary")),
        name="hgrn2",
    )(proj, proj, proj, proj, lb, gn)


def _sb_kernel(q_ref, k_ref, v_ref, u_ref, gn_ref, o_ref, later_sc, acc_sc, *, blk, heads):
    i = pl.program_id(2)
    u = u_ref[...]
    row = lax.broadcasted_iota(jnp.int32, (blk, blk), 0)
    col = lax.broadcasted_iota(jnp.int32, (blk, blk), 1)
    strict = col < row
    cols = [slice(hd * SB_D, (hd + 1) * SB_D) for hd in range(heads)]

    def blocks(kb, first):
        rows = pl.ds(pl.multiple_of(kb * blk, blk), blk)
        nzs = [_dot_nt(q_ref[:, c], k_ref[rows, c]) for c in cols]
        stage = []
        for nz in nzs:
            e = jnp.exp2(jnp.abs(nz) * -LOG2E)
            log_keep = jnp.minimum(nz, 0.0) - jnp.log(1.0 + e)
            log_beta = log_keep - nz
            if first:
                log_keep = jnp.where(strict, log_keep, 0.0)
            stage.append((log_keep.astype(BF16), _row_bcast(jnp.sum(log_keep, axis=-1, keepdims=True)),
                          log_beta))
        withins = [_dot(st[0], u) for st in stage]
        weights = []
        for hd, (st, within) in enumerate(zip(stage, withins)):
            if first:
                a = jnp.where(strict, jnp.exp(st[2] + within), 0.0)
            else:
                a = jnp.exp(st[2] + within + _tile_lanes(later_sc[hd], blk))
            weights.append(a.astype(BF16))
        for hd, (a, c) in enumerate(zip(weights, cols)):
            out = _dot(a, v_ref[rows, c])
            if first:
                later_sc[hd] = stage[hd][1]
                acc_sc[hd] = out
            else:
                later_sc[hd] += stage[hd][1]
                acc_sc[hd] += out

    def step(kb):
        blocks(kb, False)
        return 0

    blocks(i, True)
    odd = i % 2
    top = i - 1 - odd
    lax.fori_loop(0, odd, lambda j, _: step(i - 1), 0)
    lax.fori_loop(0, i // 2, lambda j, _: step(top - 2 * j) + step(top - 2 * j - 1), 0)
    for hd, c in enumerate(cols):
        o_ref[:, c] = _rms(acc_sc[hd], gn_ref[:, c]).astype(o_ref.dtype)


def _sb_attention(qkv, gn, batch, blk, heads):
    t = qkv.shape[0]
    seq = t // batch
    nq = seq // blk
    groups = SB_HEADS // heads
    width = heads * SB_D
    u = (lax.broadcasted_iota(jnp.int32, (blk, blk), 0)
         > lax.broadcasted_iota(jnp.int32, (blk, blk), 1)).astype(BF16)
    q_spec = pl.BlockSpec((blk, width), lambda b, h, i: (b * nq + i, h))
    k_spec = pl.BlockSpec((seq, width), lambda b, h, i: (b, groups + h))
    v_spec = pl.BlockSpec((seq, width), lambda b, h, i: (b, 2 * groups + h))
    return pl.pallas_call(
        functools.partial(_sb_kernel, blk=blk, heads=heads),
        grid=(batch, groups, nq),
        in_specs=[q_spec, k_spec, v_spec, pl.BlockSpec((blk, blk), lambda b, h, i: (0, 0)),
                  pl.BlockSpec((1, width), lambda b, h, i: (0, h))],
        out_specs=q_spec,
        out_shape=jax.ShapeDtypeStruct((t, SB_HEADS * SB_D), BF16),
        scratch_shapes=[pltpu.VMEM((heads, blk, LANES), F32), pltpu.VMEM((heads, blk, SB_D), F32)],
        compiler_params=_cparams(("parallel", "parallel", "arbitrary")),
        name="sb_attention",
    )(qkv, qkv, qkv, u, gn)


def _mla_kernel(q_ref, k_ref, v_ref, gn_ref, o_ref, m_sc, l_sc, acc_sc, *, blk, heads):
    i = pl.program_id(2)
    row = lax.broadcasted_iota(jnp.int32, (blk, blk), 0)
    col = lax.broadcasted_iota(jnp.int32, (blk, blk), 1)
    allowed = (col // CHUNK) <= (row // CHUNK)

    def blocks(kb, first):
        rows = pl.ds(pl.multiple_of(kb * blk, blk), blk)
        scores = [_dot_nt(q_ref[:, hd * MLA_QK_PAD:(hd + 1) * MLA_QK_PAD],
                          k_ref[rows, hd * MLA_QK_PAD:(hd + 1) * MLA_QK_PAD]) for hd in range(heads)]
        stage = []
        for hd, s in enumerate(scores):
            if first:
                s = jnp.where(allowed, s, -jnp.inf)
                m_new = _row_bcast(jnp.max(s, axis=-1, keepdims=True))
                alpha = None
            else:
                m_old = m_sc[hd]
                m_new = jnp.maximum(m_old, _row_bcast(jnp.max(s, axis=-1, keepdims=True)))
                alpha = jnp.exp2(m_old - m_new)
            p = jnp.exp2(s - _tile_lanes(m_new, blk))
            m_sc[hd] = m_new
            stage.append((alpha, _row_bcast(jnp.sum(p, axis=-1, keepdims=True)), p.astype(BF16)))
        for hd, (alpha, p_sum, p) in enumerate(stage):
            out = _dot(p, v_ref[rows, hd * MLA_V:(hd + 1) * MLA_V])
            if first:
                l_sc[hd] = p_sum
                acc_sc[hd] = out
            else:
                l_sc[hd] = alpha * l_sc[hd] + p_sum
                acc_sc[hd] = alpha * acc_sc[hd] + out

    def step(kb):
        blocks(kb, False)
        return 0

    blocks(i, True)
    odd = i % 2
    lax.fori_loop(0, odd, lambda j, _: step(0), 0)
    lax.fori_loop(0, i // 2, lambda j, _: step(odd + 2 * j) + step(odd + 2 * j + 1), 0)
    for hd in range(heads):
        cols = slice(hd * MLA_V, (hd + 1) * MLA_V)
        o_ref[:, cols] = _rms(acc_sc[hd] / l_sc[hd], gn_ref[:, cols]).astype(o_ref.dtype)


def _mla_attention(q, k, v, gn, batch, blk, heads):
    t = q.shape[0]
    seq = t // batch
    nq = seq // blk
    qk_w = heads * MLA_QK_PAD
    v_w = heads * MLA_V
    return pl.pallas_call(
        functools.partial(_mla_kernel, blk=blk, heads=heads),
        grid=(batch, MLA_HEADS // heads, nq),
        in_specs=[
            pl.BlockSpec((blk, qk_w), lambda b, h, i: (b * nq + i, h)),
            pl.BlockSpec((seq, qk_w), lambda b, h, i: (b, h)),
            pl.BlockSpec((seq, v_w), lambda b, h, i: (b, h)),
            pl.BlockSpec((1, v_w), lambda b, h, i: (0, h)),
        ],
        out_specs=pl.BlockSpec((blk, v_w), lambda b, h, i: (b * nq + i, h)),
        out_shape=jax.ShapeDtypeStruct((t, MLA_HEADS * MLA_V), BF16),
        scratch_shapes=[pltpu.VMEM((heads, blk, LANES), F32), pltpu.VMEM((heads, blk, LANES), F32),
                        pltpu.VMEM((heads, blk, MLA_V), F32)],
        compiler_params=_cparams(("parallel", "parallel", "arbitrary")),
        name="mla_attention",
    )(q, k, v, gn)


def _out_proj_kernel(x_ref, a_ref, b_ref, c_ref, w_ref, o_ref):
    ka, kb = a_ref.shape[1], b_ref.shape[1]
    o_ref[...] = (x_ref[...] + _dot(a_ref[...], w_ref[:ka]) + _dot(b_ref[...], w_ref[ka:ka + kb])
                  + _dot(c_ref[...], w_ref[ka + kb:]))


def _out_proj(x, a, b, c, w, layer, tm, tn):
    t, d = x.shape
    act = lambda arr: pl.BlockSpec((tm, arr.shape[1]), lambda i, j: (i, 0))
    tile = pl.BlockSpec((tm, tn), lambda i, j: (i, j))
    return pl.pallas_call(
        _out_proj_kernel,
        grid=(t // tm, d // tn),
        in_specs=[tile, act(a), act(b), act(c),
                  pl.BlockSpec((None, w.shape[1], tn), lambda i, j: (layer, 0, j))],
        out_specs=tile,
        out_shape=jax.ShapeDtypeStruct((t, d), F32),
        compiler_params=_cparams(("parallel", "parallel")),
        name="out_proj",
    )(x, a, b, c, w)


def _ffn_kernel(x_ref, g_ref, wg_ref, wu_ref, wd_ref, gf_ref, o_ref, h_ref, *, final_norm):
    j = pl.program_id(1)

    @pl.when(j == 0)
    def _():
        x = x_ref[...]
        h_ref[...] = _rms(x, g_ref[...]).astype(BF16)
        o_ref[...] = x

    h = h_ref[...]
    gate = _dot(h, wg_ref[...])
    up = _dot(h, wu_ref[...])
    act = (gate * (1.0 / (1.0 + jnp.exp(-gate))) * up).astype(BF16)
    o_ref[...] += _dot(act, wd_ref[...])

    if final_norm:
        @pl.when(j == pl.num_programs(1) - 1)
        def _():
            o_ref[...] = _rms(o_ref[...], gf_ref[...])


def _ffn(x, g, wg, wu, wd, gf, layer, final_norm, tm, tf):
    t, d = x.shape
    ff = wg.shape[2]
    row = pl.BlockSpec((tm, d), lambda i, j: (i, 0))
    vec = pl.BlockSpec((1, d), lambda i, j: (0, 0))
    return pl.pallas_call(
        functools.partial(_ffn_kernel, final_norm=final_norm),
        grid=(t // tm, ff // tf),
        in_specs=[row, vec, pl.BlockSpec((None, d, tf), lambda i, j: (layer, 0, j)),
                  pl.BlockSpec((None, d, tf), lambda i, j: (layer, 0, j)),
                  pl.BlockSpec((None, tf, d), lambda i, j: (layer, j, 0)), vec],
        out_specs=row,
        out_shape=jax.ShapeDtypeStruct((t, d), F32),
        scratch_shapes=[pltpu.VMEM((tm, d), BF16)],
        compiler_params=_cparams(("parallel", "arbitrary")),
        name="ffn",
    )(x, g, wg, wu, wd, gf)


def _rotate_half_cols(w):
    half = MLA_ROPE // 2
    return jnp.concatenate([-w[..., half:], w[..., :half]], axis=-1)


def _rope_tables(positions):
    inv_freq = ROPE_THETA ** (-jnp.arange(0, MLA_ROPE, 2, dtype=F32) / MLA_ROPE)
    ang = positions.astype(F32).reshape(-1, 1) * inv_freq
    cos2 = jnp.concatenate([jnp.cos(ang)] * 2, axis=-1)
    sin2 = jnp.concatenate([jnp.sin(ang)] * 2, axis=-1)
    pad = jnp.zeros((ang.shape[0], LANES - MLA_ROPE), F32)
    return jnp.concatenate([cos2, pad], axis=-1), jnp.concatenate([sin2, pad], axis=-1)


def _mla_weights(w_in_mla_t, w_uq, w_ukv):
    d = w_in_mla_t.shape[1]
    half = MLA_ROPE // 2
    w_kr = w_in_mla_t[MLA_Q_RANK + MLA_KV_RANK:]
    w_kr_rot = jnp.concatenate([-w_kr[half:], w_kr[:half]], axis=0)
    zpad = jnp.zeros((LANES - MLA_ROPE, d), w_in_mla_t.dtype)
    w_in = jnp.concatenate([w_in_mla_t[:MLA_Q_RANK + MLA_KV_RANK], w_kr, zpad, w_kr_rot, zpad], axis=0)

    uq = w_uq.reshape(MLA_Q_RANK, MLA_HEADS, MLA_NOPE + MLA_ROPE)
    nope, rope = uq[..., :MLA_NOPE], uq[..., MLA_NOPE:]
    zr = jnp.zeros((MLA_Q_RANK, MLA_HEADS, MLA_QK_PAD - MLA_NOPE - MLA_ROPE), F32)
    wa = jnp.concatenate([nope, rope, zr], axis=-1).reshape(MLA_Q_RANK, MLA_HEADS * MLA_QK_PAD)
    wb = jnp.concatenate([_rotate_half_cols(rope), zr], axis=-1).reshape(MLA_Q_RANK, MLA_HEADS * LANES)

    ukv = w_ukv.reshape(MLA_KV_RANK, MLA_HEADS, MLA_NOPE + MLA_V)
    wk = ukv[..., :MLA_NOPE].reshape(MLA_KV_RANK, MLA_HEADS * MLA_NOPE)
    wv = ukv[..., MLA_NOPE:].reshape(MLA_KV_RANK, MLA_HEADS * MLA_V)
    return tuple(w.astype(BF16) for w in (w_in, wa, wb, wk, wv))


def kernel(x, positions, attn_norm_g, w_in, hg_lower_bounds, hg_norm_g, sb_norm_g, mla_q_norm_g,
           mla_w_uq, mla_kv_norm_g, mla_w_ukv, mla_out_norm_g, w_o, ffn_norm_g, w_gate, w_up,
           w_down, final_norm_g):
    batch, seq, d_model = x.shape
    depth = w_in.shape[0]
    t = batch * seq
    hg_w = HG_HEADS * HG_D
    sb_w = SB_HEADS * SB_D
    mla_w = MLA_HEADS * MLA_V
    hg_cols = 4 * hg_w
    sb_cols = 3 * sb_w

    sm = jax.nn.softmax(hg_lower_bounds.astype(F32), axis=0)
    lower_bounds = jnp.cumsum(sm, axis=0) - sm[0:1]
    mla_q_scale = (MLA_NOPE + MLA_ROPE) ** -0.5 * LOG2E
    tables = _rope_tables(positions)

    wo, wg, wu, wd = (w.astype(BF16) for w in (w_o, w_gate, w_up, w_down))
    w_in_t = jnp.swapaxes(w_in, 1, 2).astype(BF16)

    xt = x.reshape(t, d_model)
    for layer in range(depth):
        g_attn = attn_norm_g[layer].reshape(1, d_model)
        mla_ws = _mla_weights(w_in_t[layer, hg_cols + sb_cols:], mla_w_uq[layer], mla_w_ukv[layer])

        hg_proj, sb_proj, h = _in_proj(xt, g_attn, w_in_t, layer, hg_cols, sb_cols, sb_w, -(SB_D ** -0.5),
                                       tm=min(t, 512))
        mq, mk, mv = _mla_proj(h, mla_ws[0], mla_q_norm_g[layer].reshape(1, -1), mla_ws[1],
                               mla_ws[2], mla_kv_norm_g[layer].reshape(1, -1), mla_ws[3], mla_ws[4],
                               *tables, mla_q_scale, tm=min(t, 512))

        hg_out = _hgrn(hg_proj, lower_bounds[layer].reshape(1, hg_w), hg_norm_g[layer].reshape(1, hg_w),
                       batch, tb=min(seq, 512))
        sb_out = _sb_attention(sb_proj, sb_norm_g[layer].reshape(1, sb_w), batch, blk=min(seq, 256),
                               heads=ATTN_HEADS_PER_STEP)
        mla_out = _mla_attention(mq, mk, mv, mla_out_norm_g[layer].reshape(1, mla_w), batch,
                                 blk=min(seq, 256), heads=ATTN_HEADS_PER_STEP)

        xt = _out_proj(xt, hg_out, sb_out, mla_out, wo, layer, tm=min(t, 512), tn=d_model)
        xt = _ffn(xt, ffn_norm_g[layer].reshape(1, d_model), wg, wu, wd, final_norm_g.reshape(1, d_model),
                  layer, final_norm=(layer == depth - 1), tm=min(t, 1024), tf=512)
    return xt.reshape(batch, seq, d_model)
```

```python
import functools

import jax
import jax.numpy as jnp
from jax import lax
from jax.experimental import pallas as pl
from jax.experimental.pallas import tpu as pltpu

F32 = jnp.float32
BF16 = jnp.bfloat16

NORM_EPS = 1e-6
LOG2E = 1.4426950408889634
ROPE_THETA = 10000.0
CHUNK = 64

HG_HEADS = 4
HG_D = 128
SB_HEADS = 6
SB_D = 128
MLA_HEADS = 6
MLA_NOPE = 128
MLA_ROPE = 64
MLA_V = 128
MLA_Q_RANK = 512
MLA_KV_RANK = 256
MLA_QK_PAD = 256

ATTN_HEADS_PER_STEP = 6
HG_CHUNK = 64
HG_SUB = 8
HG_UNROLL = 2
SUBLANES = 8
LANES = 128
VMEM_LIMIT = 56 * 1024 * 1024


def _cparams(semantics):
    return pltpu.CompilerParams(dimension_semantics=semantics, vmem_limit_bytes=VMEM_LIMIT)


def _rms(x, g):
    return x * lax.rsqrt(jnp.mean(x * x, axis=-1, keepdims=True) + NORM_EPS) * g


def _dot(a, b):
    return jnp.dot(a, b, preferred_element_type=F32)


def _row_bcast(col):
    return jnp.broadcast_to(col, (col.shape[0], LANES))


def _tile_lanes(x, width):
    return jnp.concatenate([x] * (width // LANES), axis=1)


def _dot_nt(a, b):
    return lax.dot_general(a, b, (((1,), (1,)), ((), ())), preferred_element_type=F32)


def _dot_tn(a, b):
    return lax.dot_general(a, b, (((0,), (0,)), ((), ())), preferred_element_type=F32)


def _in_proj_kernel(x_ref, g_ref, w_ref, hg_ref, sb_ref, h_ref, *, q_cols, q_scale):
    hg_cols = hg_ref.shape[1]
    h = _rms(x_ref[...], g_ref[...]).astype(BF16)
    h_ref[...] = h
    hg_ref[...] = _dot_nt(h, w_ref[:hg_cols])
    sb = _dot_nt(h, w_ref[hg_cols:])
    sb_ref[:, :q_cols] = (sb[:, :q_cols] * q_scale).astype(BF16)
    sb_ref[:, q_cols:] = sb[:, q_cols:].astype(BF16)


def _in_proj(x, g, w_t, layer, hg_cols, sb_cols, q_cols, q_scale, tm):
    t, d = x.shape
    row = lambda width: pl.BlockSpec((tm, width), lambda i: (i, 0))
    return pl.pallas_call(
        functools.partial(_in_proj_kernel, q_cols=q_cols, q_scale=q_scale),
        grid=(t // tm,),
        in_specs=[
            row(d),
            pl.BlockSpec((1, d), lambda i: (0, 0)),
            pl.BlockSpec((None, hg_cols + sb_cols, d), lambda i: (layer, 0, 0), pipeline_mode=pl.Buffered(1)),
        ],
        out_specs=[row(hg_cols), row(sb_cols), row(d)],
        out_shape=[jax.ShapeDtypeStruct((t, hg_cols), F32), jax.ShapeDtypeStruct((t, sb_cols), BF16),
                   jax.ShapeDtypeStruct((t, d), BF16)],
        compiler_params=_cparams(("parallel",)),
        name="in_proj",
    )(x, g, w_t)


def _mla_proj_kernel(h_ref, win_ref, gq_ref, wa_ref, wb_ref, gkv_ref, wk_ref, wv_ref,
                     cos_ref, sin_ref, q_ref, k_ref, v_ref, *, q_scale):
    p = _dot_nt(h_ref[...], win_ref[...])
    c_q = p[:, :MLA_Q_RANK]
    c_kv = p[:, MLA_Q_RANK:MLA_Q_RANK + MLA_KV_RANK]
    k_rope = p[:, MLA_Q_RANK + MLA_KV_RANK:MLA_Q_RANK + MLA_KV_RANK + LANES]
    k_rope_rot = p[:, MLA_Q_RANK + MLA_KV_RANK + LANES:]

    cqn = _rms(c_q, gq_ref[...]).astype(BF16)
    qa = _dot(cqn, wa_ref[...])
    qb = _dot(cqn, wb_ref[...])
    cos = cos_ref[...]
    sin = sin_ref[...]
    q_cos = cos * q_scale
    q_sin = sin * q_scale
    for hd in range(MLA_HEADS):
        nope = slice(hd * MLA_QK_PAD, hd * MLA_QK_PAD + MLA_NOPE)
        rope = slice(hd * MLA_QK_PAD + MLA_NOPE, (hd + 1) * MLA_QK_PAD)
        q_ref[:, nope] = (qa[:, nope] * q_scale).astype(BF16)
        q_ref[:, rope] = (qa[:, rope] * q_cos + qb[:, hd * LANES:(hd + 1) * LANES] * q_sin).astype(BF16)

    ckvn = _rms(c_kv, gkv_ref[...]).astype(BF16)
    kn = _dot(ckvn, wk_ref[...])
    v_ref[...] = _dot(ckvn, wv_ref[...]).astype(BF16)
    kr = (k_rope * cos + k_rope_rot * sin).astype(BF16)
    for hd in range(MLA_HEADS):
        k_ref[:, hd * MLA_QK_PAD:hd * MLA_QK_PAD + MLA_NOPE] = (
            kn[:, hd * MLA_NOPE:(hd + 1) * MLA_NOPE].astype(BF16))
        k_ref[:, hd * MLA_QK_PAD + MLA_NOPE:(hd + 1) * MLA_QK_PAD] = kr


def _mla_proj(h, w_in, gq, wa, wb, gkv, wk, wv, cos, sin, q_scale, tm):
    t, d = h.shape
    const = lambda i: (0, 0)
    row = lambda i: (i, 0)
    full = lambda a: pl.BlockSpec(a.shape, const)
    qk_w = MLA_HEADS * MLA_QK_PAD
    v_w = MLA_HEADS * MLA_V
    return pl.pallas_call(
        functools.partial(_mla_proj_kernel, q_scale=q_scale),
        grid=(t // tm,),
        in_specs=[
            pl.BlockSpec((tm, d), row), full(w_in), full(gq), full(wa), full(wb),
            full(gkv), full(wk), full(wv),
            pl.BlockSpec((tm, LANES), row), pl.BlockSpec((tm, LANES), row),
        ],
        out_specs=[
            pl.BlockSpec((tm, qk_w), row), pl.BlockSpec((tm, qk_w), row), pl.BlockSpec((tm, v_w), row),
        ],
        out_shape=[
            jax.ShapeDtypeStruct((t, qk_w), BF16), jax.ShapeDtypeStruct((t, qk_w), BF16),
            jax.ShapeDtypeStruct((t, v_w), BF16),
        ],
        compiler_params=_cparams(("parallel",)),
        name="mla_proj",
    )(h, w_in, gq, wa, wb, gkv, wk, wv, cos, sin)


def _hgrn_diag_scores(q, g2, h_ref):
    lane = lax.broadcasted_iota(jnp.int32, (SUBLANES, LANES), 1)
    row = lax.broadcasted_iota(jnp.int32, (SUBLANES, LANES), 0)
    out = []
    for tv in range(HG_CHUNK // SUBLANES):
        t0 = tv * SUBLANES
        s0 = (t0 // HG_SUB) * HG_SUB
        qv = q[t0:t0 + SUBLANES]
        gv = g2[t0:t0 + SUBLANES]
        acc = jnp.zeros((SUBLANES, LANES), F32)
        for s in range(s0, t0 + SUBLANES):
            m = qv * jnp.exp2(gv - h_ref[s:s + 1, :])
            acc = jnp.where(lane == s, jnp.sum(m, axis=-1, keepdims=True), acc)
        out.append(jnp.where(lane <= row + t0, acc, 0.0))
    return jnp.concatenate(out, axis=0)


def _hgrn_kernel(q_ref, f_ref, v_ref, gate_ref, lb_ref, gn_ref, o_ref, st_ref, g_slots, h_slots):
    c = HG_CHUNK
    n_sub = c // HG_SUB
    n_chunks = q_ref.shape[0] // c
    heads = range(HG_HEADS)
    cols = [slice(hd * HG_D, (hd + 1) * HG_D) for hd in heads]

    @pl.when(pl.program_id(1) == 0)
    def _():
        st_ref[...] = jnp.zeros_like(st_ref)

    tri3 = (lax.broadcasted_iota(jnp.int32, (c, 3 * c), 0)
            >= lax.broadcasted_iota(jnp.int32, (c, 3 * c), 1) % c).astype(BF16)
    zeros_sub = jnp.zeros((HG_SUB, LANES), F32)

    def chunk(ci, slot):
        rows = pl.ds(pl.multiple_of(ci * c, c), c)
        g_sc, h_sc = g_slots.at[slot], h_slots.at[slot]

        ks, pieces = [], []
        for cs in cols:
            fp = f_ref[rows, cs]
            lb = lb_ref[:, cs]
            e = jnp.exp(-jnp.abs(fp))
            r = 1.0 / (1.0 + e)
            sig = jnp.where(fp >= 0, r, e * r)
            nsig = jnp.where(fp >= 0, e * r, r)
            log_f = jnp.log(lb + (1.0 - lb) * sig)
            ks.append((1.0 - lb) * nsig)
            p1 = log_f.astype(BF16)
            r1 = log_f - p1.astype(F32)
            p2 = r1.astype(BF16)
            p3 = (r1 - p2.astype(F32)).astype(BF16)
            pieces.append(jnp.concatenate([p1, p2, p3], axis=0))
        gs = [_dot(tri3, p) for p in pieces]

        qs, vs, g2s, q_in, k_out, q_hat, k_hat = [], [], [], [], [], [], []
        for hd, cs in enumerate(cols):
            g, k = gs[hd], ks[hd]
            g2 = g * LOG2E
            g_sc[hd] = g
            h_sc[hd] = g2 - jnp.log(k) * LOG2E
            g2s.append(g2)
            q = q_ref[rows, cs]
            qs.append(q)
            vs.append(v_ref[rows, cs].astype(BF16))
            g_last = g_sc[hd, c - 1:c, :]
            q_in.append((q * jnp.exp(g)).astype(BF16))
            k_out.append((k * jnp.exp(g_last - g)).astype(BF16))
            qh, kh = [], []
            for sb in range(1, n_sub):
                n = sb * HG_SUB
                g_prev = g_sc[hd, n - 1:n, :]
                qh.append((q[n:n + HG_SUB] * jnp.exp(g[n:n + HG_SUB] - g_prev)).astype(BF16))
                kh.append(jnp.concatenate(
                    [k[:n] * jnp.exp(g_prev - g[:n])] + [zeros_sub] * (n_sub - sb), axis=0).astype(BF16))
            q_hat.append(qh)
            k_hat.append(kh)

        off, inter = [], []
        for hd in heads:
            off.append(jnp.concatenate(
                [zeros_sub[:, :c]] + [_dot_nt(qh, kh) for qh, kh in zip(q_hat[hd], k_hat[hd])], axis=0))
            st = st_ref[hd]
            inter.append(_dot_nt(q_in[hd], st.astype(BF16)))
            g_last = g_sc[hd, c - 1:c, :]
            st_ref[hd] = st * jnp.exp(g_last) + _dot_tn(vs[hd], k_out[hd])

        diag = [_hgrn_diag_scores(qs[hd], g2s[hd], h_sc.at[hd]) for hd in heads]
        outs = [_dot((diag[hd][:, :c] + off[hd]).astype(BF16), vs[hd]) + inter[hd] for hd in heads]

        for hd, cs in enumerate(cols):
            y = _rms(outs[hd], gn_ref[:, cs])
            gate = gate_ref[rows, cs]
            o_ref[rows, cs] = (y * gate * (1.0 / (1.0 + jnp.exp(-gate)))).astype(o_ref.dtype)
        return 0

    lax.fori_loop(0, n_chunks // HG_UNROLL,
                  lambda p, _: sum(chunk(HG_UNROLL * p + u, u) for u in range(HG_UNROLL)), 0)


def _hgrn(proj, lb, gn, batch, tb):
    t = proj.shape[0]
    nt = t // batch // tb
    width = HG_HEADS * HG_D
    col = lambda off: pl.BlockSpec((tb, width), lambda b, i: (b * nt + i, off))
    vec = pl.BlockSpec((1, width), lambda b, i: (0, 0))
    per_head = lambda rows: pltpu.VMEM((HG_HEADS, rows, HG_D), F32)
    slots = pltpu.VMEM((HG_UNROLL, HG_HEADS, HG_CHUNK, HG_D), F32)
    return pl.pallas_call(
        _hgrn_kernel,
        grid=(batch, nt),
        in_specs=[col(0), col(1), col(2), col(3), vec, vec],
        out_specs=pl.BlockSpec((tb, width), lambda b, i: (b * nt + i, 0)),
        out_shape=jax.ShapeDtypeStruct((t, width), BF16),
        scratch_shapes=[per_head(HG_D), slots, slots],
        compiler_params=_cparams(("parallel", "arbitrary")),
        name="hgrn2",
    )(proj, proj, proj, proj, lb, gn)


def _sb_kernel(q_ref, k_ref, v_ref, u_ref, gn_ref, o_ref, later_sc, acc_sc, *, blk, heads):
    i = pl.program_id(2)
    u = u_ref[...]
    row = lax.broadcasted_iota(jnp.int32, (blk, blk), 0)
    col = lax.broadcasted_iota(jnp.int32, (blk, blk), 1)
    strict = col < row
    cols = [slice(hd * SB_D, (hd + 1) * SB_D) for hd in range(heads)]

    def blocks(kb, first):
        rows = pl.ds(pl.multiple_of(kb * blk, blk), blk)
        nzs = [_dot_nt(q_ref[:, c], k_ref[rows, c]) for c in cols]
        stage = []
        for nz in nzs:
            e = jnp.exp2(jnp.abs(nz) * -LOG2E)
            log_keep = jnp.minimum(nz, 0.0) - jnp.log(1.0 + e)
            log_beta = log_keep - nz
            if first:
                log_keep = jnp.where(strict, log_keep, 0.0)
            stage.append((log_keep.astype(BF16), _row_bcast(jnp.sum(log_keep, axis=-1, keepdims=True)),
                          log_beta))
        withins = [_dot(st[0], u) for st in stage]
        weights = []
        for hd, (st, within) in enumerate(zip(stage, withins)):
            if first:
                a = jnp.where(strict, jnp.exp(st[2] + within), 0.0)
            else:
                a = jnp.exp(st[2] + within + _tile_lanes(later_sc[hd], blk))
            weights.append(a.astype(BF16))
        for hd, (a, c) in enumerate(zip(weights, cols)):
            out = _dot(a, v_ref[rows, c])
            if first:
                later_sc[hd] = stage[hd][1]
                acc_sc[hd] = out
            else:
                later_sc[hd] += stage[hd][1]
                acc_sc[hd] += out

    def step(kb):
        blocks(kb, False)
        return 0

    blocks(i, True)
    odd = i % 2
    top = i - 1 - odd
    lax.fori_loop(0, odd, lambda j, _: step(i - 1), 0)
    lax.fori_loop(0, i // 2, lambda j, _: step(top - 2 * j) + step(top - 2 * j - 1), 0)
    for hd, c in enumerate(cols):
        o_ref[:, c] = _rms(acc_sc[hd], gn_ref[:, c]).astype(o_ref.dtype)


def _sb_attention(qkv, gn, batch, blk, heads):
    t = qkv.shape[0]
    seq = t // batch
    nq = seq // blk
    groups = SB_HEADS // heads
    width = heads * SB_D
    u = (lax.broadcasted_iota(jnp.int32, (blk, blk), 0)
         > lax.broadcasted_iota(jnp.int32, (blk, blk), 1)).astype(BF16)
    q_spec = pl.BlockSpec((blk, width), lambda b, h, i: (b * nq + i, h))
    k_spec = pl.BlockSpec((seq, width), lambda b, h, i: (b, groups + h))
    v_spec = pl.BlockSpec((seq, width), lambda b, h, i: (b, 2 * groups + h))
    return pl.pallas_call(
        functools.partial(_sb_kernel, blk=blk, heads=heads),
        grid=(batch, groups, nq),
        in_specs=[q_spec, k_spec, v_spec, pl.BlockSpec((blk, blk), lambda b, h, i: (0, 0)),
                  pl.BlockSpec((1, width), lambda b, h, i: (0, h))],
        out_specs=q_spec,
        out_shape=jax.ShapeDtypeStruct((t, SB_HEADS * SB_D), BF16),
        scratch_shapes=[pltpu.VMEM((heads, blk, LANES), F32), pltpu.VMEM((heads, blk, SB_D), F32)],
        compiler_params=_cparams(("parallel", "parallel", "arbitrary")),
        name="sb_attention",
    )(qkv, qkv, qkv, u, gn)


def _mla_kernel(q_ref, k_ref, v_ref, gn_ref, o_ref, m_sc, l_sc, acc_sc, *, blk, heads):
    i = pl.program_id(2)
    row = lax.broadcasted_iota(jnp.int32, (blk, blk), 0)
    col = lax.broadcasted_iota(jnp.int32, (blk, blk), 1)
    allowed = (col // CHUNK) <= (row // CHUNK)

    def blocks(kb, first):
        rows = pl.ds(pl.multiple_of(kb * blk, blk), blk)
        scores = [_dot_nt(q_ref[:, hd * MLA_QK_PAD:(hd + 1) * MLA_QK_PAD],
                          k_ref[rows, hd * MLA_QK_PAD:(hd + 1) * MLA_QK_PAD]) for hd in range(heads)]
        stage = []
        for hd, s in enumerate(scores):
            if first:
                s = jnp.where(allowed, s, -jnp.inf)
                m_new = _row_bcast(jnp.max(s, axis=-1, keepdims=True))
                alpha = None
            else:
                m_old = m_sc[hd]
                m_new = jnp.maximum(m_old, _row_bcast(jnp.max(s, axis=-1, keepdims=True)))
                alpha = jnp.exp2(m_old - m_new)
            p = jnp.exp2(s - _tile_lanes(m_new, blk))
            m_sc[hd] = m_new
            stage.append((alpha, _row_bcast(jnp.sum(p, axis=-1, keepdims=True)), p.astype(BF16)))
        for hd, (alpha, p_sum, p) in enumerate(stage):
            out = _dot(p, v_ref[rows, hd * MLA_V:(hd + 1) * MLA_V])
            if first:
                l_sc[hd] = p_sum
                acc_sc[hd] = out
            else:
                l_sc[hd] = alpha * l_sc[hd] + p_sum
                acc_sc[hd] = alpha * acc_sc[hd] + out

    def step(kb):
        blocks(kb, False)
        return 0

    blocks(i, True)
    odd = i % 2
    lax.fori_loop(0, odd, lambda j, _: step(0), 0)
    lax.fori_loop(0, i // 2, lambda j, _: step(odd + 2 * j) + step(odd + 2 * j + 1), 0)
    for hd in range(heads):
        cols = slice(hd * MLA_V, (hd + 1) * MLA_V)
        o_ref[:, cols] = _rms(acc_sc[hd] / l_sc[hd], gn_ref[:, cols]).astype(o_ref.dtype)


def _mla_attention(q, k, v, gn, batch, blk, heads):
    t = q.shape[0]
    seq = t // batch
    nq = seq // blk
    qk_w = heads * MLA_QK_PAD
    v_w = heads * MLA_V
    return pl.pallas_call(
        functools.partial(_mla_kernel, blk=blk, heads=heads),
        grid=(batch, MLA_HEADS // heads, nq),
        in_specs=[
            pl.BlockSpec((blk, qk_w), lambda b, h, i: (b * nq + i, h)),
            pl.BlockSpec((seq, qk_w), lambda b, h, i: (b, h)),
            pl.BlockSpec((seq, v_w), lambda b, h, i: (b, h)),
            pl.BlockSpec((1, v_w), lambda b, h, i: (0, h)),
        ],
        out_specs=pl.BlockSpec((blk, v_w), lambda b, h, i: (b * nq + i, h)),
        out_shape=jax.ShapeDtypeStruct((t, MLA_HEADS * MLA_V), BF16),
        scratch_shapes=[pltpu.VMEM((heads, blk, LANES), F32), pltpu.VMEM((heads, blk, LANES), F32),
                        pltpu.VMEM((heads, blk, MLA_V), F32)],
        compiler_params=_cparams(("parallel", "parallel", "arbitrary")),
        name="mla_attention",
    )(q, k, v, gn)


def _out_proj_kernel(x_ref, a_ref, b_ref, c_ref, w_ref, o_ref):
    ka, kb = a_ref.shape[1], b_ref.shape[1]
    o_ref[...] = (x_ref[...] + _dot(a_ref[...], w_ref[:ka]) + _dot(b_ref[...], w_ref[ka:ka + kb])
                  + _dot(c_ref[...], w_ref[ka + kb:]))


def _out_proj(x, a, b, c, w, layer, tm, tn):
    t, d = x.shape
    act = lambda arr: pl.BlockSpec((tm, arr.shape[1]), lambda i, j: (i, 0))
    tile = pl.BlockSpec((tm, tn), lambda i, j: (i, j))
    return pl.pallas_call(
        _out_proj_kernel,
        grid=(t // tm, d // tn),
        in_specs=[tile, act(a), act(b), act(c),
                  pl.BlockSpec((None, w.shape[1], tn), lambda i, j: (layer, 0, j))],
        out_specs=tile,
        out_shape=jax.ShapeDtypeStruct((t, d), F32),
        compiler_params=_cparams(("parallel", "parallel")),
        name="out_proj",
    )(x, a, b, c, w)


def _ffn_kernel(x_ref, g_ref, wg_ref, wu_ref, wd_ref, gf_ref, o_ref, h_ref, *, final_norm):
    j = pl.program_id(1)

    @pl.when(j == 0)
    def _():
        x = x_ref[...]
        h_ref[...] = _rms(x, g_ref[...]).astype(BF16)
        o_ref[...] = x

    h = h_ref[...]
    gate = _dot(h, wg_ref[...])
    up = _dot(h, wu_ref[...])
    act = (gate * (1.0 / (1.0 + jnp.exp(-gate))) * up).astype(BF16)
    o_ref[...] += _dot(act, wd_ref[...])

    if final_norm:
        @pl.when(j == pl.num_programs(1) - 1)
        def _():
            o_ref[...] = _rms(o_ref[...], gf_ref[...])


def _ffn(x, g, wg, wu, wd, gf, layer, final_norm, tm, tf):
    t, d = x.shape
    ff = wg.shape[2]
    row = pl.BlockSpec((tm, d), lambda i, j: (i, 0))
    vec = pl.BlockSpec((1, d), lambda i, j: (0, 0))
    return pl.pallas_call(
        functools.partial(_ffn_kernel, final_norm=final_norm),
        grid=(t // tm, ff // tf),
        in_specs=[row, vec, pl.BlockSpec((None, d, tf), lambda i, j: (layer, 0, j)),
                  pl.BlockSpec((None, d, tf), lambda i, j: (layer, 0, j)),
                  pl.BlockSpec((None, tf, d), lambda i, j: (layer, j, 0)), vec],
        out_specs=row,
        out_shape=jax.ShapeDtypeStruct((t, d), F32),
        scratch_shapes=[pltpu.VMEM((tm, d), BF16)],
        compiler_params=_cparams(("parallel", "arbitrary")),
        name="ffn",
    )(x, g, wg, wu, wd, gf)


def _rotate_half_cols(w):
    half = MLA_ROPE // 2
    return jnp.concatenate([-w[..., half:], w[..., :half]], axis=-1)


def _rope_tables(positions):
    inv_freq = ROPE_THETA ** (-jnp.arange(0, MLA_ROPE, 2, dtype=F32) / MLA_ROPE)
    freq = jnp.concatenate([inv_freq, inv_freq, jnp.zeros((LANES - MLA_ROPE,), F32)])
    ang = positions.astype(F32).reshape(-1, 1) * freq
    live = jnp.arange(LANES) < MLA_ROPE
    return jnp.where(live, jnp.cos(ang), 0.0), jnp.sin(ang)


def _mla_weights(w_in_mla_t, w_uq, w_ukv):
    d = w_in_mla_t.shape[1]
    half = MLA_ROPE // 2
    w_kr = w_in_mla_t[MLA_Q_RANK + MLA_KV_RANK:]
    w_kr_rot = jnp.concatenate([-w_kr[half:], w_kr[:half]], axis=0)
    zpad = jnp.zeros((LANES - MLA_ROPE, d), w_in_mla_t.dtype)
    w_in = jnp.concatenate([w_in_mla_t[:MLA_Q_RANK + MLA_KV_RANK], w_kr, zpad, w_kr_rot, zpad], axis=0)

    uq = w_uq.reshape(MLA_Q_RANK, MLA_HEADS, MLA_NOPE + MLA_ROPE)
    nope, rope = uq[..., :MLA_NOPE], uq[..., MLA_NOPE:]
    zr = jnp.zeros((MLA_Q_RANK, MLA_HEADS, MLA_QK_PAD - MLA_NOPE - MLA_ROPE), F32)
    wa = jnp.concatenate([nope, rope, zr], axis=-1).reshape(MLA_Q_RANK, MLA_HEADS * MLA_QK_PAD)
    wb = jnp.concatenate([_rotate_half_cols(rope), zr], axis=-1).reshape(MLA_Q_RANK, MLA_HEADS * LANES)

    ukv = w_ukv.reshape(MLA_KV_RANK, MLA_HEADS, MLA_NOPE + MLA_V)
    wk = ukv[..., :MLA_NOPE].reshape(MLA_KV_RANK, MLA_HEADS * MLA_NOPE)
    wv = ukv[..., MLA_NOPE:].reshape(MLA_KV_RANK, MLA_HEADS * MLA_V)
    return tuple(w.astype(BF16) for w in (w_in, wa, wb, wk, wv))


def kernel(x, positions, attn_norm_g, w_in, hg_lower_bounds, hg_norm_g, sb_norm_g, mla_q_norm_g,
           mla_w_uq, mla_kv_norm_g, mla_w_ukv, mla_out_norm_g, w_o, ffn_norm_g, w_gate, w_up,
           w_down, final_norm_g):
    batch, seq, d_model = x.shape
    depth = w_in.shape[0]
    t = batch * seq
    hg_w = HG_HEADS * HG_D
    sb_w = SB_HEADS * SB_D
    mla_w = MLA_HEADS * MLA_V
    hg_cols = 4 * hg_w
    sb_cols = 3 * sb_w

    sm = jax.nn.softmax(hg_lower_bounds.astype(F32), axis=0)
    lower_bounds = jnp.cumsum(sm, axis=0) - sm[0:1]
    mla_q_scale = (MLA_NOPE + MLA_ROPE) ** -0.5 * LOG2E
    tables = _rope_tables(positions)

    wo, wg, wu, wd = (w.astype(BF16) for w in (w_o, w_gate, w_up, w_down))
    w_in_t = jnp.swapaxes(w_in, 1, 2).astype(BF16)

    xt = x.reshape(t, d_model)
    for layer in range(depth):
        g_attn = attn_norm_g[layer].reshape(1, d_model)
        mla_ws = _mla_weights(w_in_t[layer, hg_cols + sb_cols:], mla_w_uq[layer], mla_w_ukv[layer])

        hg_proj, sb_proj, h = _in_proj(xt, g_attn, w_in_t, layer, hg_cols, sb_cols, sb_w, -(SB_D ** -0.5),
                                       tm=min(t, 512))
        mq, mk, mv = _mla_proj(h, mla_ws[0], mla_q_norm_g[layer].reshape(1, -1), mla_ws[1],
                               mla_ws[2], mla_kv_norm_g[layer].reshape(1, -1), mla_ws[3], mla_ws[4],
                               *tables, mla_q_scale, tm=min(t, 512))

        hg_out = _hgrn(hg_proj, lower_bounds[layer].reshape(1, hg_w), hg_norm_g[layer].reshape(1, hg_w),
                       batch, tb=min(seq, 1024))
        sb_out = _sb_attention(sb_proj, sb_norm_g[layer].reshape(1, sb_w), batch, blk=min(seq, 256),
                               heads=ATTN_HEADS_PER_STEP)
        mla_out = _mla_attention(mq, mk, mv, mla_out_norm_g[layer].reshape(1, mla_w), batch,
                                 blk=min(seq, 256), heads=ATTN_HEADS_PER_STEP)

        xt = _out_proj(xt, hg_out, sb_out, mla_out, wo, layer, tm=min(t, 512), tn=d_model)
        xt = _ffn(xt, ffn_norm_g[layer].reshape(1, d_model), wg, wu, wd, final_norm_g.reshape(1, d_model),
                  layer, final_norm=(layer == depth - 1), tm=min(t, 1024), tf=512)
    return xt.reshape(batch, seq, d_model)
```
